```python
import jax, jax.numpy as jnp
from jax import lax
import numpy as np

D_MODEL = 1024
BATCH = 32
SEQ = 2048
DEPTH = 2
DEC_BATCH = 1
DEC_SEQ = 16384
PAST_LEN = 128

N_FGROUPS = 4
FGROUP = 128
F_WIDTH = N_FGROUPS * FGROUP
HEAD_DIM = 128
N_HEADS = 8
N_KV = 2
GROUP = N_HEADS // N_KV
A_WIDTH = N_HEADS * HEAD_DIM
KV_WIDTH = N_KV * HEAD_DIM
WINDOW = 128
BLOCK = 128
ROT_DIM = HEAD_DIM // 4
ROPE_THETA = 500000.0
EPS = 1e-6
NEG = -1e30
SPLIT_SIZES = (F_WIDTH, F_WIDTH, A_WIDTH, KV_WIDTH, KV_WIDTH, A_WIDTH, 2 * D_MODEL)
IN_WIDTH = sum(SPLIT_SIZES)
SPLIT_POINTS = tuple(int(v) for v in np.cumsum(SPLIT_SIZES)[:-1])

kernel_name = "hybrid_fnet_swa_gated_encoder"


def rmsnorm(x, g):
    xf = x.astype(jnp.float32)
    y = xf * lax.rsqrt(jnp.mean(xf * xf, axis=-1, keepdims=True) + EPS)
    return (y * g.astype(jnp.float32)).astype(x.dtype)


def rope_partial(x, pos):
    half = ROT_DIM // 2
    inv = 1.0 / (jnp.float32(ROPE_THETA) ** (jnp.arange(half, dtype=jnp.float32) * 2.0 / ROT_DIM))
    ang = pos.astype(jnp.float32)[:, None] * inv[None, :]
    cos = jnp.cos(ang)[None, :, None, :]
    sin = jnp.sin(ang)[None, :, None, :]
    xr = x[..., :ROT_DIM].astype(jnp.float32)
    x1, x2 = xr[..., :half], xr[..., half:]
    rot = jnp.concatenate([x1 * cos - x2 * sin, x2 * cos + x1 * sin], axis=-1)
    return jnp.concatenate([rot.astype(x.dtype), x[..., ROT_DIM:]], axis=-1)


def fourier_mix(u):
    B, S, _ = u.shape
    ug = u.reshape(B, S, N_FGROUPS, FGROUP).astype(jnp.float32)
    y = jnp.fft.fft2(ug, axes=(1, 3), norm="ortho").real
    return y.reshape(B, S, F_WIDTH).astype(u.dtype)


def window_attention(q, k, v, sink):
    B, S, _, _ = q.shape
    nb = S // BLOCK
    qb = q.reshape(B, nb, BLOCK, N_KV, GROUP, HEAD_DIM)

    def band(t):
        tp = jnp.pad(t, ((0, 0), (BLOCK, BLOCK), (0, 0), (0, 0)))
        tb = tp.reshape(B, nb + 2, BLOCK, N_KV, HEAD_DIM)
        return jnp.concatenate([tb[:, :-2], tb[:, 1:-1], tb[:, 2:]], axis=2)

    kb, vb = band(k), band(v)
    scale = HEAD_DIM ** -0.5
    s = jnp.einsum('bnqkgd,bnskd->bnkgqs', qb, kb).astype(jnp.float32) * scale
    a = jnp.arange(BLOCK)[:, None]
    j = jnp.arange(3 * BLOCK)[None, :]
    rel = j - BLOCK - a
    kpos = (jnp.arange(nb)[:, None] - 1) * BLOCK + jnp.arange(3 * BLOCK)[None, :]
    valid = (jnp.abs(rel) <= WINDOW)[None] & ((kpos >= 0) & (kpos < S))[:, None, :]
    s = jnp.where(valid[None, :, None, None], s, NEG)
    sk = sink.astype(jnp.float32).reshape(N_KV, GROUP)[None, None, :, :, None]
    m = jnp.maximum(jnp.max(s, axis=-1), sk)
    p = jnp.exp(s - m[..., None])
    denom = jnp.sum(p, axis=-1) + jnp.exp(sk - m)
    p = (p / denom[..., None]).astype(v.dtype)
    o = jnp.einsum('bnkgqs,bnskd->bnqkgd', p, vb)
    return o.reshape(B, S, A_WIDTH)


def layer(x, g, w_in, b_gate, sink, w_pf, w_pa, w_o):
    B, S, _ = x.shape
    h = rmsnorm(x, g)
    proj = h @ w_in
    u_f, z_f, q, k, v, z_a, gates = jnp.split(proj, SPLIT_POINTS, axis=-1)
    gates = jax.nn.sigmoid(gates + b_gate)
    g_f, g_a = gates[..., :D_MODEL], gates[..., D_MODEL:]
    y_f = fourier_mix(u_f) * jax.nn.silu(z_f)
    br_f = y_f @ w_pf
    pos = jnp.arange(S)
    q = rope_partial(q.reshape(B, S, N_HEADS, HEAD_DIM), pos)
    k = rope_partial(k.reshape(B, S, N_KV, HEAD_DIM), pos)
    v = v.reshape(B, S, N_KV, HEAD_DIM)
    y_a = window_attention(q, k, v, sink) * jax.nn.silu(z_a)
    br_a = y_a @ w_pa
    merged = g_f * br_f + g_a * br_a
    return x + merged @ w_o


def trunk(x, norm_gain, w_in, gate_bias, sink_logit, w_proj_fourier, w_proj_attn, w_out, final_norm_gain):
    for l in range(DEPTH):
        x = layer(x, norm_gain[l], w_in[l], gate_bias[l], sink_logit[l],
                  w_proj_fourier[l], w_proj_attn[l], w_out[l])
    return rmsnorm(x, final_norm_gain)


def setup_inputs(seed: int = 0) -> dict:
    key = jax.random.key(seed)
    ks = jax.random.split(key, 10)
    f32 = jnp.float32
    return {
        "x_prompt": jax.random.normal(ks[0], (BATCH, SEQ, D_MODEL), f32),
        "x_sample": jax.random.normal(ks[1], (DEC_BATCH, DEC_SEQ, D_MODEL), f32),
        "norm_gain": 1.0 + 0.01 * jax.random.normal(ks[2], (DEPTH, D_MODEL), f32),
        "w_in": jax.random.normal(ks[3], (DEPTH, D_MODEL, IN_WIDTH), f32) * D_MODEL ** -0.5,
        "gate_bias": 0.01 * jax.random.normal(ks[4], (DEPTH, 2 * D_MODEL), f32),
        "sink_logit": 0.5 * jax.random.normal(ks[5], (DEPTH, N_HEADS), f32),
        "w_proj_fourier": jax.random.normal(ks[6], (DEPTH, F_WIDTH, D_MODEL), f32) * F_WIDTH ** -0.5,
        "w_proj_attn": jax.random.normal(ks[7], (DEPTH, A_WIDTH, D_MODEL), f32) * A_WIDTH ** -0.5,
        "w_out": jax.random.normal(ks[8], (DEPTH, D_MODEL, D_MODEL), f32) * D_MODEL ** -0.5,
        "final_norm_gain": 1.0 + 0.01 * jax.random.normal(ks[9], (D_MODEL,), f32),
    }


def reference(x_prompt, x_sample, norm_gain, w_in, gate_bias, sink_logit, w_proj_fourier, w_proj_attn, w_out, final_norm_gain):
    y_prompt = trunk(x_prompt, norm_gain, w_in, gate_bias, sink_logit, w_proj_fourier, w_proj_attn, w_out, final_norm_gain)
    y_sample = trunk(x_sample, norm_gain, w_in, gate_bias, sink_logit, w_proj_fourier, w_proj_attn, w_out, final_norm_gain)
    return (y_prompt, y_sample)
```

```python
import functools
import math

import numpy as np
import jax
import jax.numpy as jnp
from jax import lax
from jax.experimental import pallas as pl
from jax.experimental.pallas import tpu as pltpu

F32 = jnp.float32
BF16 = jnp.bfloat16

D_MODEL = 1024
N_FGROUPS = 4
FGROUP = 128
F_WIDTH = N_FGROUPS * FGROUP
HEAD_DIM = 128
N_HEADS = 8
N_KV = 2
GROUP = N_HEADS // N_KV
A_WIDTH = N_HEADS * HEAD_DIM
KV_WIDTH = N_KV * HEAD_DIM
WINDOW = 128
BLOCK = 128
ROT_DIM = HEAD_DIM // 4
ROPE_THETA = 500000.0
EPS = 1e-6
NEG = -1e30
GATE_WIDTH = 2 * D_MODEL
OFF_UF = 0
OFF_ZF = OFF_UF + F_WIDTH
OFF_Q = OFF_ZF + F_WIDTH
OFF_K = OFF_Q + A_WIDTH
OFF_V = OFF_K + KV_WIDTH
OFF_ZA = OFF_V + KV_WIDTH
OFF_G = OFF_ZA + A_WIDTH
IN_WIDTH = OFF_G + GATE_WIDTH

LANES = 128
SEQ_DFT = 2048
TOKEN_TILE = 512
DFT_ROW_TILE = 512
Q_TILE = 512
VMEM_LIMIT = 56 * 1024 * 1024


def _silu(x):
    return x * jax.nn.sigmoid(x)


def _rope_tables(seq, tile):
    half = ROT_DIM // 2
    inv = 1.0 / (ROPE_THETA ** (np.arange(half, dtype=np.float64) * 2.0 / ROT_DIM))
    lane_inv = np.zeros((LANES,), np.float64)
    lane_inv[:half] = inv
    lane_inv[half:ROT_DIM] = inv
    rows = np.arange(tile, dtype=np.float64)[:, None] * lane_inv[None, :]
    bases = (np.arange(seq // tile, dtype=np.float64) * tile)[:, None] * lane_inv[None, :]
    bases = np.repeat(bases[:, None, :], 8, axis=1)
    return (jnp.asarray(np.cos(rows), F32), jnp.asarray(np.sin(rows), F32),
            jnp.asarray(np.cos(bases), F32), jnp.asarray(np.sin(bases), F32))


def _inproj_kernel(x_ref, gain_ref, w_ref, bias_ref, cr_ref, sr_ref, cb_ref, sb_ref,
                   uf_ref, zf_ref, q_ref, k_ref, v_ref, za_ref, g_ref):
    x = x_ref[...]
    ms = jnp.mean(x * x, axis=-1, keepdims=True)
    h = ((x * lax.rsqrt(ms + EPS)) * gain_ref[...]).astype(BF16)

    def proj(c0, n):
        return jnp.dot(h, w_ref[:, c0:c0 + n], preferred_element_type=F32)

    cb = cb_ref[0, 0:1, :]
    sb = sb_ref[0, 0:1, :]
    cr = cr_ref[...]
    sr = sr_ref[...]
    cos = cb * cr - sb * sr
    sin = sb * cr + cb * sr
    lane = lax.broadcasted_iota(jnp.int32, (1, LANES), 1)
    sin_lo = jnp.where(lane < ROT_DIM // 2, -sin, 0.0)
    sin_hi = jnp.where(lane >= ROT_DIM // 2, sin, 0.0)

    def rope(v):
        return (v * cos + pltpu.roll(v, LANES - ROT_DIM // 2, 1) * sin_lo
                + pltpu.roll(v, ROT_DIM // 2, 1) * sin_hi)

    uf_ref[...] = proj(OFF_UF, F_WIDTH).astype(BF16)
    zf_ref[...] = _silu(proj(OFF_ZF, F_WIDTH)).astype(BF16)
    scale = HEAD_DIM ** -0.5
    for c in range(A_WIDTH // 512):
        qc = proj(OFF_Q + 512 * c, 512)
        for hh in range(512 // HEAD_DIM):
            lo = hh * HEAD_DIM
            q_ref[:, 512 * c + lo:512 * c + lo + HEAD_DIM] = (
                rope(qc[:, lo:lo + HEAD_DIM]) * scale).astype(BF16)
    kv = proj(OFF_K, 2 * KV_WIDTH)
    for hh in range(N_KV):
        lo = hh * HEAD_DIM
        k_ref[:, lo:lo + HEAD_DIM] = rope(kv[:, lo:lo + HEAD_DIM]).astype(BF16)
    v_ref[...] = kv[:, KV_WIDTH:].astype(BF16)
    for c in range(A_WIDTH // 512):
        za_ref[:, 512 * c:512 * (c + 1)] = _silu(proj(OFF_ZA + 512 * c, 512)).astype(BF16)
    for c in range(GATE_WIDTH // 512):
        gc = proj(OFF_G + 512 * c, 512) + bias_ref[:, 512 * c:512 * (c + 1)]
        g_ref[:, 512 * c:512 * (c + 1)] = jax.nn.sigmoid(gc).astype(BF16)


def _in_projection(x2, gain, w_in, bias, seq):
    tokens = x2.shape[0]
    tm = TOKEN_TILE
    tiles_per_seq = seq // tm
    cr, sr, cb, sb = _rope_tables(seq, tm)
    row = lambda i: (i, 0)
    const2 = lambda i: (0, 0)
    widths = (F_WIDTH, F_WIDTH, A_WIDTH, KV_WIDTH, KV_WIDTH, A_WIDTH, GATE_WIDTH)
    return pl.pallas_call(
        _inproj_kernel,
        grid=(tokens // tm,),
        in_specs=[
            pl.BlockSpec((tm, D_MODEL), row),
            pl.BlockSpec((1, D_MODEL), const2),
            pl.BlockSpec((D_MODEL, IN_WIDTH), const2),
            pl.BlockSpec((1, GATE_WIDTH), const2),
            pl.BlockSpec((tm, LANES), const2),
            pl.BlockSpec((tm, LANES), const2),
            pl.BlockSpec((1, 8, LANES), lambda i: (i % tiles_per_seq, 0, 0)),
            pl.BlockSpec((1, 8, LANES), lambda i: (i % tiles_per_seq, 0, 0)),
        ],
        out_specs=[pl.BlockSpec((tm, w), row) for w in widths],
        out_shape=[jax.ShapeDtypeStruct((tokens, w), BF16) for w in widths],
        compiler_params=pltpu.CompilerParams(
            dimension_semantics=("arbitrary",), vmem_limit_bytes=VMEM_LIMIT),
        name="in_projection",
    )(x2, gain.reshape(1, D_MODEL), w_in, bias.reshape(1, GATE_WIDTH), cr, sr, cb, sb)


def _dft_tables(seq):
    n = SEQ_DFT
    radix = seq // n
    idx = np.arange(n, dtype=np.int64)
    ang = 2.0 * np.pi * ((idx[:, None] * idx[None, :]) % n).astype(np.float64) / n
    cs = np.stack([np.cos(ang), np.sin(ang)])
    cidx = np.arange(FGROUP, dtype=np.int64)
    cang = 2.0 * np.pi * ((cidx[:, None] * cidx[None, :]) % FGROUP).astype(np.float64) / FGROUP
    norm = 1.0 / math.sqrt(float(seq) * FGROUP)
    chan = np.concatenate([np.cos(cang), np.sin(cang)], axis=0) * norm
    n1 = np.arange(radix, dtype=np.int64)
    tang = 2.0 * np.pi * ((n1[:, None] * idx[None, :]) % seq).astype(np.float64) / seq
    tw = np.stack([np.cos(tang), np.sin(tang)])
    tw = np.repeat(tw[..., None], LANES, axis=-1)
    return (jnp.asarray(cs, F32).astype(BF16), jnp.asarray(chan, F32).astype(BF16),
            jnp.asarray(tw, F32))


def _radix2_fft(re, im):
    n = len(re)
    if n == 1:
        return re, im
    er, ei = _radix2_fft(re[0::2], im[0::2])
    orr, oi = _radix2_fft(re[1::2], im[1::2])
    out_r = [None] * n
    out_i = [None] * n
    for k in range(n // 2):
        ang = -2.0 * math.pi * k / n
        c, s = math.cos(ang), math.sin(ang)
        if k == 0:
            tr, ti = orr[k], oi[k]
        elif 4 * k == n:
            tr, ti = oi[k], -orr[k]
        else:
            tr = orr[k] * c - oi[k] * s
            ti = orr[k] * s + oi[k] * c
        out_r[k] = er[k] + tr
        out_i[k] = ei[k] + ti
        out_r[k + n // 2] = er[k] - tr
        out_i[k + n // 2] = ei[k] - ti
    return out_r, out_i


def _fourier_kernel(radix, tr, cs_ref, chan_ref, tw_ref, u_ref, zf_ref, out_ref):
    r0 = pl.multiple_of(pl.program_id(1) * tr, tr)
    c = cs_ref[0, pl.ds(r0, tr), :]
    s = cs_ref[1, pl.ds(r0, tr), :]
    u = u_ref[0]
    p_all = jnp.dot(c, u, preferred_element_type=F32)
    q_all = jnp.dot(s, u, preferred_element_type=F32)
    re, im = [], []
    for n1 in range(radix):
        p = p_all[:, n1 * F_WIDTH:(n1 + 1) * F_WIDTH]
        q = q_all[:, n1 * F_WIDTH:(n1 + 1) * F_WIDTH]
        if n1 == 0:
            re.append(p)
            im.append(-q)
        else:
            ct = jnp.concatenate([tw_ref[0, n1]] * N_FGROUPS, axis=1)
            st = jnp.concatenate([tw_ref[1, n1]] * N_FGROUPS, axis=1)
            re.append(p * ct - q * st)
            im.append(-(p * st + q * ct))
    xr, xi = _radix2_fft(re, im)
    chan = chan_ref[...]
    for k1 in range(radix):
        xrb = xr[k1].astype(BF16)
        xib = xi[k1].astype(BF16)
        cols = []
        for g in range(N_FGROUPS):
            lo = g * FGROUP
            lhs = jnp.concatenate([xrb[:, lo:lo + FGROUP], xib[:, lo:lo + FGROUP]], axis=1)
            cols.append(jnp.dot(lhs, chan, preferred_element_type=F32))
        y = jnp.concatenate(cols, axis=1)
        out_ref[0, k1] = (y * zf_ref[0, k1].astype(F32)).astype(BF16)


def _fourier_mix(u, zf_silu, batch, seq):
    n = SEQ_DFT
    radix = seq // n
    cs, chan, tw = _dft_tables(seq)
    u3 = u.reshape(batch, n, radix * F_WIDTH)
    z4 = zf_silu.reshape(batch, radix, n, F_WIDTH)
    tr = DFT_ROW_TILE // radix
    u_mode = pl.Buffered(1) if batch == 1 else pl.Buffered(2)
    out = pl.pallas_call(
        functools.partial(_fourier_kernel, radix, tr),
        grid=(batch, n // tr),
        in_specs=[
            pl.BlockSpec((2, n, n), lambda b, r: (0, 0, 0), pipeline_mode=pl.Buffered(1)),
            pl.BlockSpec((2 * FGROUP, FGROUP), lambda b, r: (0, 0)),
            pl.BlockSpec((2, radix, tr, LANES), lambda b, r: (0, 0, r, 0)),
            pl.BlockSpec((1, n, radix * F_WIDTH), lambda b, r: (b, 0, 0), pipeline_mode=u_mode),
            pl.BlockSpec((1, radix, tr, F_WIDTH), lambda b, r: (b, 0, r, 0)),
        ],
        out_specs=pl.BlockSpec((1, radix, tr, F_WIDTH), lambda b, r: (b, 0, r, 0)),
        out_shape=jax.ShapeDtypeStruct((batch, radix, n, F_WIDTH), BF16),
        compiler_params=pltpu.CompilerParams(
            dimension_semantics=("arbitrary", "arbitrary"), vmem_limit_bytes=VMEM_LIMIT),
        name="fourier_mix",
    )(cs, chan, tw, u3, z4)
    return out.reshape(batch * seq, F_WIDTH)


def _attention_kernel(sink_ref, q_ref, kp_ref, kc_ref, kn_ref, vp_ref, vc_ref, vn_ref, za_ref,
                      out_ref):
    tq = Q_TILE
    blocks = tq // BLOCK
    i = pl.program_id(1)
    n_blocks = pl.num_programs(1) * blocks
    kcat = jnp.concatenate([kp_ref[0], kc_ref[0], kn_ref[0]], axis=0)
    vcat = jnp.concatenate([vp_ref[0], vc_ref[0], vn_ref[0]], axis=0)
    rows = GROUP * BLOCK
    a = lax.broadcasted_iota(jnp.int32, (rows, 3 * BLOCK), 0) % BLOCK
    col = lax.broadcasted_iota(jnp.int32, (rows, 3 * BLOCK), 1)
    rel = col - BLOCK - a
    band = jnp.abs(rel) <= WINDOW
    for j in range(blocks):
        bj = i * blocks + j
        valid = band & ((col >= BLOCK) | (bj > 0)) & ((col < 2 * BLOCK) | (bj < n_blocks - 1))
        for g in range(N_KV):
            kk = kcat[j * BLOCK:(j + 3) * BLOCK, g * HEAD_DIM:(g + 1) * HEAD_DIM]
            vv = vcat[j * BLOCK:(j + 3) * BLOCK, g * HEAD_DIM:(g + 1) * HEAD_DIM]
            heads = [GROUP * g + hh for hh in range(GROUP)]
            q4 = jnp.concatenate(
                [q_ref[0, j * BLOCK:(j + 1) * BLOCK, h * HEAD_DIM:(h + 1) * HEAD_DIM] for h in heads],
                axis=0)
            s = lax.dot_general(q4, kk, (((1,), (1,)), ((), ())), preferred_element_type=F32)
            s = jnp.where(valid, s, NEG)
            sk = jnp.concatenate(
                [jnp.full((BLOCK, 1), sink_ref[h], F32) for h in heads], axis=0)
            m = jnp.maximum(jnp.max(s, axis=-1, keepdims=True), sk)
            p = jnp.exp(s - m)
            denom = jnp.sum(p, axis=-1, keepdims=True) + jnp.exp(sk - m)
            o = jnp.dot(p.astype(BF16), vv, preferred_element_type=F32) / denom
            for hh, h in enumerate(heads):
                za = za_ref[0, j * BLOCK:(j + 1) * BLOCK, h * HEAD_DIM:(h + 1) * HEAD_DIM]
                out_ref[0, j * BLOCK:(j + 1) * BLOCK, h * HEAD_DIM:(h + 1) * HEAD_DIM] = (
                    o[hh * BLOCK:(hh + 1) * BLOCK] * za.astype(F32)).astype(BF16)


def _window_attention(q, k, v, za_silu, sink, batch, seq):
    tq = Q_TILE
    per = tq // BLOCK
    last = seq // BLOCK - 1
    q3 = q.reshape(batch, seq, A_WIDTH)
    k3 = k.reshape(batch, seq, KV_WIDTH)
    v3 = v.reshape(batch, seq, KV_WIDTH)
    z3 = za_silu.reshape(batch, seq, A_WIDTH)
    prev = pl.BlockSpec((1, BLOCK, KV_WIDTH), lambda b, i: (b, jnp.maximum(i * per - 1, 0), 0))
    cur = pl.BlockSpec((1, tq, KV_WIDTH), lambda b, i: (b, i, 0))
    nxt = pl.BlockSpec((1, BLOCK, KV_WIDTH), lambda b, i: (b, jnp.minimum((i + 1) * per, last), 0))
    wide = pl.BlockSpec((1, tq, A_WIDTH), lambda b, i: (b, i, 0))
    out = pl.pallas_call(
        _attention_kernel,
        grid=(batch, seq // tq),
        in_specs=[pl.BlockSpec(memory_space=pltpu.SMEM), wide, prev, cur, nxt, prev, cur, nxt, wide],
        out_specs=wide,
        out_shape=jax.ShapeDtypeStruct((batch, seq, A_WIDTH), BF16),
        compiler_params=pltpu.CompilerParams(
            dimension_semantics=("arbitrary", "arbitrary"), vmem_limit_bytes=VMEM_LIMIT),
        name="window_attention",
    )(sink, q3, k3, k3, k3, v3, v3, v3, z3)
    return out.reshape(batch * seq, A_WIDTH)


def _output_kernel(final, yf_ref, ya_ref, g_ref, x_ref, wpf_ref, wpa_ref, wo_ref, fgain_ref,
                   out_ref):
    br_f = jnp.dot(yf_ref[...], wpf_ref[...], preferred_element_type=F32)
    br_a = jnp.dot(ya_ref[...], wpa_ref[...], preferred_element_type=F32)
    merged = (g_ref[:, :D_MODEL].astype(F32) * br_f + g_ref[:, D_MODEL:].astype(F32) * br_a)
    y = x_ref[...] + jnp.dot(merged.astype(BF16), wo_ref[...], preferred_element_type=F32)
    if final:
        ms = jnp.mean(y * y, axis=-1, keepdims=True)
        y = (y * lax.rsqrt(ms + EPS)) * fgain_ref[...]
    out_ref[...] = y


def _output_projection(yf, ya, gates, x2, w_pf, w_pa, w_o, final_gain, final):
    tokens = x2.shape[0]
    tm = TOKEN_TILE
    row = lambda i: (i, 0)
    const2 = lambda i: (0, 0)
    return pl.pallas_call(
        functools.partial(_output_kernel, final),
        grid=(tokens // tm,),
        in_specs=[
            pl.BlockSpec((tm, F_WIDTH), row),
            pl.BlockSpec((tm, A_WIDTH), row),
            pl.BlockSpec((tm, GATE_WIDTH), row),
            pl.BlockSpec((tm, D_MODEL), row),
            pl.BlockSpec((F_WIDTH, D_MODEL), const2),
            pl.BlockSpec((A_WIDTH, D_MODEL), const2),
            pl.BlockSpec((D_MODEL, D_MODEL), const2),
            pl.BlockSpec((1, D_MODEL), const2),
        ],
        out_specs=pl.BlockSpec((tm, D_MODEL), row),
        out_shape=jax.ShapeDtypeStruct((tokens, D_MODEL), F32),
        compiler_params=pltpu.CompilerParams(
            dimension_semantics=("arbitrary",), vmem_limit_bytes=VMEM_LIMIT),
        name="output_projection",
    )(yf, ya, gates, x2, w_pf, w_pa, w_o, final_gain.reshape(1, D_MODEL))


def _trunk(x, norm_gain, w_in, gate_bias, sink_logit, w_pf, w_pa, w_o, final_gain):
    batch, seq, _ = x.shape
    depth = norm_gain.shape[0]
    x2 = x.reshape(batch * seq, D_MODEL)
    for l in range(depth):
        uf, zf, q, k, v, za, gates = _in_projection(x2, norm_gain[l], w_in[l], gate_bias[l], seq)
        yf = _fourier_mix(uf, zf, batch, seq)
        ya = _window_attention(q, k, v, za, sink_logit[l], batch, seq)
        x2 = _output_projection(yf, ya, gates, x2, w_pf[l], w_pa[l], w_o[l], final_gain,
                                final=(l == depth - 1))
    return x2.reshape(batch, seq, D_MODEL)


def kernel(x_prompt, x_sample, norm_gain, w_in, gate_bias, sink_logit, w_proj_fourier, w_proj_attn,
           w_out, final_norm_gain):
    w_in_b = w_in.astype(BF16)
    w_pf_b = w_proj_fourier.astype(BF16)
    w_pa_b = w_proj_attn.astype(BF16)
    w_o_b = w_out.astype(BF16)
    args = (norm_gain, w_in_b, gate_bias, sink_logit, w_pf_b, w_pa_b, w_o_b, final_norm_gain)
    return (_trunk(x_prompt, *args), _trunk(x_sample, *args))
```

```python
import functools
import math

import numpy as np
import jax
import jax.numpy as jnp
from jax import lax
from jax.experimental import pallas as pl
from jax.experimental.pallas import tpu as pltpu

F32 = jnp.float32
BF16 = jnp.bfloat16

D_MODEL = 1024
N_FGROUPS = 4
FGROUP = 128
F_WIDTH = N_FGROUPS * FGROUP
HEAD_DIM = 128
N_HEADS = 8
N_KV = 2
GROUP = N_HEADS // N_KV
A_WIDTH = N_HEADS * HEAD_DIM
KV_WIDTH = N_KV * HEAD_DIM
WINDOW = 128
BLOCK = 128
ROT_DIM = HEAD_DIM // 4
ROPE_THETA = 500000.0
EPS = 1e-6
NEG = -1e30
GATE_WIDTH = 2 * D_MODEL
OFF_UF = 0
OFF_ZF = OFF_UF + F_WIDTH
OFF_Q = OFF_ZF + F_WIDTH
OFF_K = OFF_Q + A_WIDTH
OFF_V = OFF_K + KV_WIDTH
OFF_ZA = OFF_V + KV_WIDTH
OFF_G = OFF_ZA + A_WIDTH
IN_WIDTH = OFF_G + GATE_WIDTH

LOG2E = math.log2(math.e)
ONES_ROWS = 16

LANES = 128
SEQ_DFT = 2048
TOKEN_TILE = 1024
DFT_ROW_TILE = 512
RADIX_ROW_TILE = 128
Q_TILE = 512
VMEM_LIMIT = 56 * 1024 * 1024


def _silu(x):
    return x * jax.nn.sigmoid(x)


def _rope_tables(seq, tile):
    half = ROT_DIM // 2
    inv = 1.0 / (ROPE_THETA ** (np.arange(half, dtype=np.float64) * 2.0 / ROT_DIM))
    lane_inv = np.zeros((LANES,), np.float64)
    lane_inv[:half] = inv
    lane_inv[half:ROT_DIM] = inv
    rows = np.arange(tile, dtype=np.float64)[:, None] * lane_inv[None, :]
    bases = (np.arange(seq // tile, dtype=np.float64) * tile)[:, None] * lane_inv[None, :]
    bases = np.repeat(bases[:, None, :], 8, axis=1)
    return (jnp.asarray(np.cos(rows), F32), jnp.asarray(np.sin(rows), F32),
            jnp.asarray(np.cos(bases), F32), jnp.asarray(np.sin(bases), F32))


def _inproj_kernel(x_ref, gain_ref, w_ref, bias_ref, cr_ref, sr_ref, cb_ref, sb_ref,
                   uf_ref, zf_ref, q_ref, k_ref, v_ref, za_ref, g_ref):
    x = x_ref[...]
    ms = jnp.mean(x * x, axis=-1, keepdims=True)
    h = ((x * lax.rsqrt(ms + EPS)) * gain_ref[...]).astype(BF16)

    def proj(c0, n):
        return jnp.dot(h, w_ref[:, c0:c0 + n], preferred_element_type=F32)

    cb = cb_ref[0, 0:1, :]
    sb = sb_ref[0, 0:1, :]
    cr = cr_ref[...]
    sr = sr_ref[...]
    cos = cb * cr - sb * sr
    sin = sb * cr + cb * sr
    lane = lax.broadcasted_iota(jnp.int32, (1, LANES), 1)
    sin_lo = jnp.where(lane < ROT_DIM // 2, -sin, 0.0)
    sin_hi = jnp.where(lane >= ROT_DIM // 2, sin, 0.0)

    def rope(v):
        return (v * cos + pltpu.roll(v, LANES - ROT_DIM // 2, 1) * sin_lo
                + pltpu.roll(v, ROT_DIM // 2, 1) * sin_hi)

    uf_ref[...] = proj(OFF_UF, F_WIDTH).astype(BF16)
    zf_ref[...] = _silu(proj(OFF_ZF, F_WIDTH)).astype(BF16)
    scale = HEAD_DIM ** -0.5 * LOG2E
    for c in range(A_WIDTH // 512):
        qc = proj(OFF_Q + 512 * c, 512)
        for hh in range(512 // HEAD_DIM):
            lo = hh * HEAD_DIM
            q_ref[:, 512 * c + lo:512 * c + lo + HEAD_DIM] = (
                rope(qc[:, lo:lo + HEAD_DIM]) * scale).astype(BF16)
    kv = proj(OFF_K, 2 * KV_WIDTH)
    for hh in range(N_KV):
        lo = hh * HEAD_DIM
        k_ref[:, lo:lo + HEAD_DIM] = rope(kv[:, lo:lo + HEAD_DIM]).astype(BF16)
    v_ref[...] = kv[:, KV_WIDTH:].astype(BF16)
    for c in range(A_WIDTH // 512):
        za_ref[:, 512 * c:512 * (c + 1)] = _silu(proj(OFF_ZA + 512 * c, 512)).astype(BF16)
    for c in range(GATE_WIDTH // 512):
        gc = proj(OFF_G + 512 * c, 512) + bias_ref[:, 512 * c:512 * (c + 1)]
        g_ref[:, 512 * c:512 * (c + 1)] = jax.nn.sigmoid(gc).astype(BF16)


def _in_projection(x2, gain, w_in, bias, seq):
    tokens = x2.shape[0]
    tm = TOKEN_TILE
    tiles_per_seq = seq // tm
    cr, sr, cb, sb = _rope_tables(seq, tm)
    row = lambda i: (i, 0)
    const2 = lambda i: (0, 0)
    widths = (F_WIDTH, F_WIDTH, A_WIDTH, KV_WIDTH, KV_WIDTH, A_WIDTH, GATE_WIDTH)
    return pl.pallas_call(
        _inproj_kernel,
        grid=(tokens // tm,),
        in_specs=[
            pl.BlockSpec((tm, D_MODEL), row),
            pl.BlockSpec((1, D_MODEL), const2),
            pl.BlockSpec((D_MODEL, IN_WIDTH), const2, pipeline_mode=pl.Buffered(1)),
            pl.BlockSpec((1, GATE_WIDTH), const2),
            pl.BlockSpec((tm, LANES), const2),
            pl.BlockSpec((tm, LANES), const2),
            pl.BlockSpec((1, 8, LANES), lambda i: (i % tiles_per_seq, 0, 0)),
            pl.BlockSpec((1, 8, LANES), lambda i: (i % tiles_per_seq, 0, 0)),
        ],
        out_specs=[pl.BlockSpec((tm, w), row) for w in widths],
        out_shape=[jax.ShapeDtypeStruct((tokens, w), BF16) for w in widths],
        compiler_params=pltpu.CompilerParams(
            dimension_semantics=("arbitrary",), vmem_limit_bytes=VMEM_LIMIT),
        name="in_projection",
    )(x2, gain.reshape(1, D_MODEL), w_in, bias.reshape(1, GATE_WIDTH), cr, sr, cb, sb)


def _dft_tables(seq):
    n = SEQ_DFT
    radix = seq // n
    idx = np.arange(n, dtype=np.int64)
    ang = 2.0 * np.pi * ((idx[:, None] * idx[None, :]) % n).astype(np.float64) / n
    cs = np.stack([np.cos(ang), np.sin(ang)])
    cidx = np.arange(FGROUP, dtype=np.int64)
    cang = 2.0 * np.pi * ((cidx[:, None] * cidx[None, :]) % FGROUP).astype(np.float64) / FGROUP
    norm = 1.0 / math.sqrt(float(seq) * FGROUP)
    chan = np.concatenate([np.cos(cang), np.sin(cang)], axis=0) * norm
    n1 = np.arange(radix, dtype=np.int64)
    tang = 2.0 * np.pi * ((n1[:, None] * idx[None, :]) % seq).astype(np.float64) / seq
    tw = np.stack([np.cos(tang), np.sin(tang)])
    tw = np.repeat(tw[..., None], LANES, axis=-1)
    return (jnp.asarray(cs, F32).astype(BF16), jnp.asarray(chan, F32).astype(BF16),
            jnp.asarray(tw, F32))


def _radix2_fft(re, im):
    n = len(re)
    if n == 1:
        return re, im
    er, ei = _radix2_fft(re[0::2], im[0::2])
    orr, oi = _radix2_fft(re[1::2], im[1::2])
    out_r = [None] * n
    out_i = [None] * n
    for k in range(n // 2):
        ang = -2.0 * math.pi * k / n
        c, s = math.cos(ang), math.sin(ang)
        if k == 0:
            tr, ti = orr[k], oi[k]
        elif 4 * k == n:
            tr, ti = oi[k], -orr[k]
        else:
            tr = orr[k] * c - oi[k] * s
            ti = orr[k] * s + oi[k] * c
        out_r[k] = er[k] + tr
        out_i[k] = ei[k] + ti
        out_r[k + n // 2] = er[k] - tr
        out_i[k + n // 2] = ei[k] - ti
    return out_r, out_i


def _fourier_kernel(radix, tr, cs_ref, chan_ref, tw_ref, u_ref, zf_ref, out_ref):
    r0 = pl.multiple_of(pl.program_id(1) * tr, tr)
    c = cs_ref[0, pl.ds(r0, tr), :]
    s = cs_ref[1, pl.ds(r0, tr), :]
    u = u_ref[0]
    p_all = jnp.dot(c, u, preferred_element_type=F32)
    q_all = jnp.dot(s, u, preferred_element_type=F32)
    re, im = [], []
    for n1 in range(radix):
        p = p_all[:, n1 * F_WIDTH:(n1 + 1) * F_WIDTH]
        q = q_all[:, n1 * F_WIDTH:(n1 + 1) * F_WIDTH]
        if n1 == 0:
            re.append(p)
            im.append(-q)
        else:
            ct = jnp.concatenate([tw_ref[0, n1]] * N_FGROUPS, axis=1)
            st = jnp.concatenate([tw_ref[1, n1]] * N_FGROUPS, axis=1)
            re.append(p * ct - q * st)
            im.append(-(p * st + q * ct))
    xr, xi = _radix2_fft(re, im)
    chan = chan_ref[...]
    for k1 in range(radix):
        xrb = xr[k1].astype(BF16)
        xib = xi[k1].astype(BF16)
        cols = []
        for g in range(N_FGROUPS):
            lo = g * FGROUP
            lhs = jnp.concatenate([xrb[:, lo:lo + FGROUP], xib[:, lo:lo + FGROUP]], axis=1)
            cols.append(jnp.dot(lhs, chan, preferred_element_type=F32))
        y = jnp.concatenate(cols, axis=1)
        out_ref[0, k1] = (y * zf_ref[0, k1].astype(F32)).astype(BF16)


def _fourier_mix(u, zf_silu, batch, seq):
    n = SEQ_DFT
    radix = seq // n
    cs, chan, tw = _dft_tables(seq)
    u3 = u.reshape(batch, n, radix * F_WIDTH)
    z4 = zf_silu.reshape(batch, radix, n, F_WIDTH)
    tr = DFT_ROW_TILE if radix == 1 else RADIX_ROW_TILE
    u_mode = pl.Buffered(1) if batch == 1 else pl.Buffered(2)
    out = pl.pallas_call(
        functools.partial(_fourier_kernel, radix, tr),
        grid=(batch, n // tr),
        in_specs=[
            pl.BlockSpec((2, n, n), lambda b, r: (0, 0, 0), pipeline_mode=pl.Buffered(1)),
            pl.BlockSpec((2 * FGROUP, FGROUP), lambda b, r: (0, 0)),
            pl.BlockSpec((2, radix, tr, LANES), lambda b, r: (0, 0, r, 0)),
            pl.BlockSpec((1, n, radix * F_WIDTH), lambda b, r: (b, 0, 0), pipeline_mode=u_mode),
            pl.BlockSpec((1, radix, tr, F_WIDTH), lambda b, r: (b, 0, r, 0)),
        ],
        out_specs=pl.BlockSpec((1, radix, tr, F_WIDTH), lambda b, r: (b, 0, r, 0)),
        out_shape=jax.ShapeDtypeStruct((batch, radix, n, F_WIDTH), BF16),
        compiler_params=pltpu.CompilerParams(
            dimension_semantics=("arbitrary", "arbitrary"), vmem_limit_bytes=VMEM_LIMIT),
        name="fourier_mix",
    )(cs, chan, tw, u3, z4)
    return out.reshape(batch * seq, F_WIDTH)


def _attention_kernel(sink_ref, q_ref, kp_ref, kc_ref, kn_ref, vp_ref, vc_ref, vn_ref, za_ref,
                      out_ref):
    tq = Q_TILE
    blocks = tq // BLOCK
    i = pl.program_id(1)
    n_blocks = pl.num_programs(1) * blocks
    kcat = jnp.concatenate([kp_ref[0], kc_ref[0], kn_ref[0]], axis=0)
    vcat = jnp.concatenate([vp_ref[0], vc_ref[0], vn_ref[0]], axis=0)
    vcat_t = vcat.astype(F32).T.astype(BF16)
    ones_rows = jnp.ones((ONES_ROWS, 3 * BLOCK), BF16)
    d = (lax.broadcasted_iota(jnp.int32, (BLOCK, BLOCK), 0)
         - lax.broadcasted_iota(jnp.int32, (BLOCK, BLOCK), 1))
    d = jnp.concatenate([d] * GROUP, axis=1)
    for j in range(blocks):
        bj = i * blocks + j
        lo = jnp.where(bj > 0, 0, BLOCK)
        hi = jnp.where(bj < n_blocks - 1, 0, -BLOCK)
        for g in range(N_KV):
            heads = [GROUP * g + hh for hh in range(GROUP)]
            kk = kcat[j * BLOCK:(j + 3) * BLOCK, g * HEAD_DIM:(g + 1) * HEAD_DIM]
            v_t = vcat_t[g * HEAD_DIM:(g + 1) * HEAD_DIM, j * BLOCK:(j + 3) * BLOCK]
            v_aug = jnp.concatenate([v_t, ones_rows], axis=0)
            q4 = jnp.concatenate(
                [q_ref[0, j * BLOCK:(j + 1) * BLOCK, h * HEAD_DIM:(h + 1) * HEAD_DIM] for h in heads],
                axis=0)
            s = lax.dot_general(kk, q4, (((1,), (1,)), ((), ())), preferred_element_type=F32)
            top = jnp.where(d >= lo, s[:BLOCK], NEG)
            mid = s[BLOCK:2 * BLOCK]
            bot = jnp.where(d <= hi, s[2 * BLOCK:], NEG)
            sk = jnp.concatenate(
                [jnp.full((1, BLOCK), sink_ref[h] * LOG2E, F32) for h in heads], axis=1)
            m = jnp.maximum(jnp.maximum(jnp.max(top, axis=0, keepdims=True),
                                        jnp.max(mid, axis=0, keepdims=True)),
                            jnp.maximum(jnp.max(bot, axis=0, keepdims=True), sk))
            p = jnp.concatenate([jnp.exp2(top - m), jnp.exp2(mid - m), jnp.exp2(bot - m)], axis=0)
            o_aug = jnp.dot(v_aug, p.astype(BF16), preferred_element_type=F32)
            denom = o_aug[HEAD_DIM:HEAD_DIM + 1] + jnp.exp2(sk - m)
            o_t = o_aug[:HEAD_DIM] * (1.0 / denom)
            for hh, h in enumerate(heads):
                o = o_t[:, hh * BLOCK:(hh + 1) * BLOCK].T
                za = za_ref[0, j * BLOCK:(j + 1) * BLOCK, h * HEAD_DIM:(h + 1) * HEAD_DIM]
                out_ref[0, j * BLOCK:(j + 1) * BLOCK, h * HEAD_DIM:(h + 1) * HEAD_DIM] = (
                    o * za.astype(F32)).astype(BF16)


def _window_attention(q, k, v, za_silu, sink, batch, seq):
    tq = Q_TILE
    per = tq // BLOCK
    last = seq // BLOCK - 1
    q3 = q.reshape(batch, seq, A_WIDTH)
    k3 = k.reshape(batch, seq, KV_WIDTH)
    v3 = v.reshape(batch, seq, KV_WIDTH)
    z3 = za_silu.reshape(batch, seq, A_WIDTH)
    prev = pl.BlockSpec((1, BLOCK, KV_WIDTH), lambda b, i: (b, jnp.maximum(i * per - 1, 0), 0))
    cur = pl.BlockSpec((1, tq, KV_WIDTH), lambda b, i: (b, i, 0))
    nxt = pl.BlockSpec((1, BLOCK, KV_WIDTH), lambda b, i: (b, jnp.minimum((i + 1) * per, last), 0))
    wide = pl.BlockSpec((1, tq, A_WIDTH), lambda b, i: (b, i, 0))
    out = pl.pallas_call(
        _attention_kernel,
        grid=(batch, seq // tq),
        in_specs=[pl.BlockSpec(memory_space=pltpu.SMEM), wide, prev, cur, nxt, prev, cur, nxt, wide],
        out_specs=wide,
        out_shape=jax.ShapeDtypeStruct((batch, seq, A_WIDTH), BF16),
        compiler_params=pltpu.CompilerParams(
            dimension_semantics=("arbitrary", "arbitrary"), vmem_limit_bytes=VMEM_LIMIT),
        name="window_attention",
    )(sink, q3, k3, k3, k3, v3, v3, v3, z3)
    return out.reshape(batch * seq, A_WIDTH)


def _output_kernel(final, yf_ref, ya_ref, g_ref, x_ref, wpf_ref, wpa_ref, wo_ref, fgain_ref,
                   out_ref):
    br_f = jnp.dot(yf_ref[...], wpf_ref[...], preferred_element_type=F32)
    br_a = jnp.dot(ya_ref[...], wpa_ref[...], preferred_element_type=F32)
    merged = (g_ref[:, :D_MODEL].astype(F32) * br_f + g_ref[:, D_MODEL:].astype(F32) * br_a)
    y = x_ref[...] + jnp.dot(merged.astype(BF16), wo_ref[...], preferred_element_type=F32)
    if final:
        ms = jnp.mean(y * y, axis=-1, keepdims=True)
        y = (y * lax.rsqrt(ms + EPS)) * fgain_ref[...]
    out_ref[...] = y


def _output_projection(yf, ya, gates, x2, w_pf, w_pa, w_o, final_gain, final):
    tokens = x2.shape[0]
    tm = TOKEN_TILE
    row = lambda i: (i, 0)
    const2 = lambda i: (0, 0)
    return pl.pallas_call(
        functools.partial(_output_kernel, final),
        grid=(tokens // tm,),
        in_specs=[
            pl.BlockSpec((tm, F_WIDTH), row),
            pl.BlockSpec((tm, A_WIDTH), row),
            pl.BlockSpec((tm, GATE_WIDTH), row),
            pl.BlockSpec((tm, D_MODEL), row),
            pl.BlockSpec((F_WIDTH, D_MODEL), const2, pipeline_mode=pl.Buffered(1)),
            pl.BlockSpec((A_WIDTH, D_MODEL), const2, pipeline_mode=pl.Buffered(1)),
            pl.BlockSpec((D_MODEL, D_MODEL), const2, pipeline_mode=pl.Buffered(1)),
            pl.BlockSpec((1, D_MODEL), const2),
        ],
        out_specs=pl.BlockSpec((tm, D_MODEL), row),
        out_shape=jax.ShapeDtypeStruct((tokens, D_MODEL), F32),
        compiler_params=pltpu.CompilerParams(
            dimension_semantics=("arbitrary",), vmem_limit_bytes=VMEM_LIMIT),
        name="output_projection",
    )(yf, ya, gates, x2, w_pf, w_pa, w_o, final_gain.reshape(1, D_MODEL))


def _trunk(x, norm_gain, w_in, gate_bias, sink_logit, w_pf, w_pa, w_o, final_gain):
    batch, seq, _ = x.shape
    depth = norm_gain.shape[0]
    x2 = x.reshape(batch * seq, D_MODEL)
    for l in range(depth):
        uf, zf, q, k, v, za, gates = _in_projection(x2, norm_gain[l], w_in[l], gate_bias[l], seq)
        yf = _fourier_mix(uf, zf, batch, seq)
        ya = _window_attention(q, k, v, za, sink_logit[l], batch, seq)
        x2 = _output_projection(yf, ya, gates, x2, w_pf[l], w_pa[l], w_o[l], final_gain,
                                final=(l == depth - 1))
    return x2.reshape(batch, seq, D_MODEL)


def kernel(x_prompt, x_sample, norm_gain, w_in, gate_bias, sink_logit, w_proj_fourier, w_proj_attn,
           w_out, final_norm_gain):
    w_in_b = w_in.astype(BF16)
    w_pf_b = w_proj_fourier.astype(BF16)
    w_pa_b = w_proj_attn.astype(BF16)
    w_o_b = w_out.astype(BF16)
    args = (norm_gain, w_in_b, gate_bias, sink_logit, w_pf_b, w_pa_b, w_o_b, final_norm_gain)
    return (_trunk(x_prompt, *args), _trunk(x_sample, *args))
```

```python
import functools
import math

import numpy as np
import jax
import jax.numpy as jnp
from jax import lax
from jax.experimental import pallas as pl
from jax.experimental.pallas import tpu as pltpu

F32 = jnp.float32
BF16 = jnp.bfloat16

D_MODEL = 1024
N_FGROUPS = 4
FGROUP = 128
F_WIDTH = N_FGROUPS * FGROUP
HEAD_DIM = 128
N_HEADS = 8
N_KV = 2
GROUP = N_HEADS // N_KV
A_WIDTH = N_HEADS * HEAD_DIM
KV_WIDTH = N_KV * HEAD_DIM
WINDOW = 128
BLOCK = 128
ROT_DIM = HEAD_DIM // 4
ROPE_THETA = 500000.0
EPS = 1e-6
NEG = -1e30
GATE_WIDTH = 2 * D_MODEL
OFF_UF = 0
OFF_ZF = OFF_UF + F_WIDTH
OFF_Q = OFF_ZF + F_WIDTH
OFF_K = OFF_Q + A_WIDTH
OFF_V = OFF_K + KV_WIDTH
OFF_ZA = OFF_V + KV_WIDTH
OFF_G = OFF_ZA + A_WIDTH
IN_WIDTH = OFF_G + GATE_WIDTH

LOG2E = math.log2(math.e)
ONES_ROWS = 16

LANES = 128
SEQ_DFT = 1024
HALF_DFT = SEQ_DFT // 2
DFT_PAD = 16
REV_BLOCK = 128
TOKEN_TILE = 1024
DFT_ROW_TILE = 512
RADIX_ROW_TILE = 128
Q_TILE = 1024
VMEM_LIMIT = 56 * 1024 * 1024


def _silu(x):
    return x * jax.nn.sigmoid(x)


def _rope_tables(seq, tile):
    half = ROT_DIM // 2
    inv = 1.0 / (ROPE_THETA ** (np.arange(half, dtype=np.float64) * 2.0 / ROT_DIM))
    lane_inv = np.zeros((LANES,), np.float64)
    lane_inv[:half] = inv
    lane_inv[half:ROT_DIM] = inv
    rows = np.arange(tile, dtype=np.float64)[:, None] * lane_inv[None, :]
    bases = (np.arange(seq // tile, dtype=np.float64) * tile)[:, None] * lane_inv[None, :]
    bases = np.repeat(bases[:, None, :], 8, axis=1)
    return (jnp.asarray(np.cos(rows), F32), jnp.asarray(np.sin(rows), F32),
            jnp.asarray(np.cos(bases), F32), jnp.asarray(np.sin(bases), F32))


def _inproj_kernel(radix, x_ref, gain_ref, w_ref, bias_ref, cr_ref, sr_ref, cb_ref, sb_ref,
                   uf_ref, zf_ref, q_ref, k_ref, v_ref, za_ref, g_ref, u_scratch):
    x = x_ref[...]
    ms = jnp.mean(x * x, axis=-1, keepdims=True)
    h = ((x * lax.rsqrt(ms + EPS)) * gain_ref[...]).astype(BF16)

    def proj(c0, n):
        return jnp.dot(h, w_ref[:, c0:c0 + n], preferred_element_type=F32)

    cb = cb_ref[0, 0:1, :]
    sb = sb_ref[0, 0:1, :]
    cr = cr_ref[...]
    sr = sr_ref[...]
    cos = cb * cr - sb * sr
    sin = sb * cr + cb * sr
    lane = lax.broadcasted_iota(jnp.int32, (1, LANES), 1)
    sin_lo = jnp.where(lane < ROT_DIM // 2, -sin, 0.0)
    sin_hi = jnp.where(lane >= ROT_DIM // 2, sin, 0.0)

    def rope(v):
        return (v * cos + pltpu.roll(v, LANES - ROT_DIM // 2, 1) * sin_lo
                + pltpu.roll(v, ROT_DIM // 2, 1) * sin_hi)

    u = proj(OFF_UF, F_WIDTH)
    rows = u.shape[0] // radix
    for g in range(N_FGROUPS):
        u_scratch[g] = u[:, g * FGROUP:(g + 1) * FGROUP]
    for n1 in range(radix):
        for g in range(N_FGROUPS):
            lo = n1 * F_WIDTH + g * FGROUP
            uf_ref[:, lo:lo + FGROUP] = (
                u_scratch[g, pl.ds(n1, rows, stride=radix), :].astype(BF16))
    zf_ref[...] = _silu(proj(OFF_ZF, F_WIDTH)).astype(BF16)
    scale = HEAD_DIM ** -0.5 * LOG2E
    for c in range(A_WIDTH // 512):
        qc = proj(OFF_Q + 512 * c, 512)
        for hh in range(512 // HEAD_DIM):
            lo = hh * HEAD_DIM
            q_ref[:, 512 * c + lo:512 * c + lo + HEAD_DIM] = (
                rope(qc[:, lo:lo + HEAD_DIM]) * scale).astype(BF16)
    kv = proj(OFF_K, 2 * KV_WIDTH)
    for hh in range(N_KV):
        lo = hh * HEAD_DIM
        k_ref[:, lo:lo + HEAD_DIM] = rope(kv[:, lo:lo + HEAD_DIM]).astype(BF16)
    v_ref[...] = kv[:, KV_WIDTH:].astype(BF16)
    for c in range(A_WIDTH // 512):
        za_ref[:, 512 * c:512 * (c + 1)] = _silu(proj(OFF_ZA + 512 * c, 512)).astype(BF16)
    for c in range(GATE_WIDTH // 512):
        gc = proj(OFF_G + 512 * c, 512) + bias_ref[:, 512 * c:512 * (c + 1)]
        g_ref[:, 512 * c:512 * (c + 1)] = jax.nn.sigmoid(gc).astype(BF16)


def _in_projection(x2, gain, w_in, bias, seq):
    tokens = x2.shape[0]
    tm = TOKEN_TILE
    tiles_per_seq = seq // tm
    cr, sr, cb, sb = _rope_tables(seq, tm)
    row = lambda i: (i, 0)
    const2 = lambda i: (0, 0)
    radix = seq // SEQ_DFT
    widths = (F_WIDTH, A_WIDTH, KV_WIDTH, KV_WIDTH, A_WIDTH, GATE_WIDTH)
    return pl.pallas_call(
        functools.partial(_inproj_kernel, radix),
        grid=(tokens // tm,),
        in_specs=[
            pl.BlockSpec((tm, D_MODEL), row),
            pl.BlockSpec((1, D_MODEL), const2),
            pl.BlockSpec((D_MODEL, IN_WIDTH), const2, pipeline_mode=pl.Buffered(1)),
            pl.BlockSpec((1, GATE_WIDTH), const2),
            pl.BlockSpec((tm, LANES), const2),
            pl.BlockSpec((tm, LANES), const2),
            pl.BlockSpec((1, 8, LANES), lambda i: (i % tiles_per_seq, 0, 0)),
            pl.BlockSpec((1, 8, LANES), lambda i: (i % tiles_per_seq, 0, 0)),
        ],
        out_specs=([pl.BlockSpec((tm // radix, radix * F_WIDTH), row)]
                   + [pl.BlockSpec((tm, w), row) for w in widths]),
        out_shape=([jax.ShapeDtypeStruct((tokens // radix, radix * F_WIDTH), BF16)]
                   + [jax.ShapeDtypeStruct((tokens, w), BF16) for w in widths]),
        scratch_shapes=[pltpu.VMEM((N_FGROUPS, tm, FGROUP), F32)],
        compiler_params=pltpu.CompilerParams(
            dimension_semantics=("arbitrary",), vmem_limit_bytes=VMEM_LIMIT),
        name="in_projection",
    )(x2, gain.reshape(1, D_MODEL), w_in, bias.reshape(1, GATE_WIDTH), cr, sr, cb, sb)


def _dft_tables(seq, tr):
    n = SEQ_DFT
    radix = seq // n
    tiles = HALF_DFT // tr
    rows = tr + DFT_PAD
    k = np.arange(HALF_DFT + DFT_PAD, dtype=np.int64)
    m = np.arange(n, dtype=np.int64)
    ang = 2.0 * np.pi * ((k[:, None] * m[None, :]) % n).astype(np.float64) / n
    cs = np.stack([np.cos(ang), np.sin(ang)])
    cidx = np.arange(FGROUP, dtype=np.int64)
    cang = 2.0 * np.pi * ((cidx[:, None] * cidx[None, :]) % FGROUP).astype(np.float64) / FGROUP
    cc, sc = np.cos(cang), np.sin(cang)
    norm = 1.0 / math.sqrt(float(seq) * FGROUP)
    chan = np.block([[cc, cc], [sc, -sc]]) * norm
    rev_n = min(REV_BLOCK, tr)
    rev = np.zeros((rev_n, rev_n), np.float64)
    rev[np.arange(1, rev_n), rev_n - np.arange(1, rev_n)] = 1.0
    n1 = np.arange(radix, dtype=np.int64)
    i = np.arange(rows, dtype=np.int64)
    tang = 2.0 * np.pi * (n1[:, None] * i[None, :]).astype(np.float64) / seq
    tw = np.repeat(np.stack([np.cos(tang), np.sin(tang)])[..., None], LANES, axis=-1)
    t0 = np.arange(tiles, dtype=np.int64) * tr
    bang = 2.0 * np.pi * ((t0[:, None] * n1[None, :]) % seq).astype(np.float64) / seq
    base = np.stack([np.cos(bang), np.sin(bang)]).reshape(-1)
    return (jnp.asarray(cs, F32).astype(BF16), jnp.asarray(chan, F32).astype(BF16),
            jnp.asarray(rev, F32).astype(BF16), jnp.asarray(tw, F32), jnp.asarray(base, F32))


def _radix2_fft(re, im):
    n = len(re)
    if n == 1:
        return re, im
    er, ei = _radix2_fft(re[0::2], im[0::2])
    orr, oi = _radix2_fft(re[1::2], im[1::2])
    out_r = [None] * n
    out_i = [None] * n
    for k in range(n // 2):
        ang = -2.0 * math.pi * k / n
        c, s = math.cos(ang), math.sin(ang)
        if k == 0:
            tr, ti = orr[k], oi[k]
        elif 4 * k == n:
            tr, ti = oi[k], -orr[k]
        else:
            tr = orr[k] * c - oi[k] * s
            ti = orr[k] * s + oi[k] * c
        out_r[k] = er[k] + tr
        out_i[k] = ei[k] + ti
        out_r[k + n // 2] = er[k] - tr
        out_i[k + n // 2] = ei[k] - ti
    return out_r, out_i


def _fourier_kernel(radix, tr, tiles, base_ref, cs_ref, chan_ref, rev_ref, tw_ref, u_ref,
                    zlo_ref, zhi_ref, lo_ref, hi_ref):
    t = pl.program_id(1)
    rows = tr + DFT_PAD
    r0 = pl.multiple_of(t * tr, tr)
    c = cs_ref[0, pl.ds(r0, rows), :]
    s = cs_ref[1, pl.ds(r0, rows), :]
    chan = chan_ref[...]
    rev = rev_ref[...]
    rev_n = rev.shape[0]
    pass_groups = N_FGROUPS if radix <= 2 else 2
    width = pass_groups * FGROUP
    first_row = lax.broadcasted_iota(jnp.int32, (8, width), 0) == 0
    for g0 in range(0, N_FGROUPS, pass_groups):
        c0 = g0 * FGROUP
        re, im = [], []
        for n1 in range(radix):
            u = u_ref[0, :, n1 * F_WIDTH + c0:n1 * F_WIDTH + c0 + width]
            p = jnp.dot(c, u, preferred_element_type=F32)
            q = jnp.dot(s, u, preferred_element_type=F32)
            if n1 == 0:
                re.append(p)
                im.append(-q)
                continue
            ct, st = tw_ref[0, n1], tw_ref[1, n1]
            if tiles > 1:
                cb = base_ref[t * radix + n1]
                sb = base_ref[(tiles + t) * radix + n1]
                ct, st = cb * ct - sb * st, sb * ct + cb * st
            ct = jnp.concatenate([ct] * pass_groups, axis=1)
            st = jnp.concatenate([st] * pass_groups, axis=1)
            re.append(p * ct - q * st)
            im.append(-(p * st + q * ct))
        xr, xi = _radix2_fft(re, im)
        for k1 in range(radix):
            xrb = xr[k1].astype(BF16)
            xib = xi[k1].astype(BF16)
            y_lo, y_hi = [], []
            for g in range(pass_groups):
                lo = g * FGROUP
                lhs = jnp.concatenate([xrb[:, lo:lo + FGROUP], xib[:, lo:lo + FGROUP]], axis=1)
                z = jnp.dot(lhs, chan, preferred_element_type=F32)
                y_lo.append(z[:tr, :FGROUP])
                y_hi.append(z[:, FGROUP:])
            y_lo = jnp.concatenate(y_lo, axis=1)
            lo_ref[0, k1, :, c0:c0 + width] = (
                y_lo * zlo_ref[0, k1, 0, :, c0:c0 + width].astype(F32)).astype(BF16)
            y_hi = jnp.concatenate(y_hi, axis=1).astype(BF16)
            km = radix - 1 - k1
            for b in range(tr // rev_n):
                top = tr - rev_n * b
                blk = jnp.dot(rev, y_hi[top - rev_n:top], preferred_element_type=F32)
                head = jnp.where(first_row, y_hi[top:top + 16].astype(F32)[:8], blk[:8])
                blk = jnp.concatenate([head, blk[8:]], axis=0)
                z_hi = zhi_ref[0, km, 0, rev_n * b:rev_n * (b + 1), c0:c0 + width].astype(F32)
                hi_ref[0, km, rev_n * b:rev_n * (b + 1), c0:c0 + width] = (blk * z_hi).astype(BF16)


def _fourier_mix(u_il, zf_silu, batch, seq):
    n = SEQ_DFT
    radix = seq // n
    tr = DFT_ROW_TILE if radix <= 2 else RADIX_ROW_TILE
    tiles = HALF_DFT // tr
    cs, chan, rev, tw, base = _dft_tables(seq, tr)
    u3 = u_il.reshape(batch, n, radix * F_WIDTH)
    z5 = zf_silu.reshape(batch, radix, 2, HALF_DFT, F_WIDTH)
    u_mode = pl.Buffered(1) if batch == 1 else pl.Buffered(2)
    const = lambda nd: (lambda b, t: (0,) * nd)
    half_shape = jax.ShapeDtypeStruct((batch, radix, HALF_DFT, F_WIDTH), BF16)
    lo, hi = pl.pallas_call(
        functools.partial(_fourier_kernel, radix, tr, tiles),
        grid=(batch, tiles),
        in_specs=[
            pl.BlockSpec(memory_space=pltpu.SMEM),
            pl.BlockSpec(cs.shape, const(3), pipeline_mode=pl.Buffered(1)),
            pl.BlockSpec(chan.shape, const(2)),
            pl.BlockSpec(rev.shape, const(2)),
            pl.BlockSpec(tw.shape, const(4), pipeline_mode=pl.Buffered(1)),
            pl.BlockSpec((1, n, radix * F_WIDTH), lambda b, t: (b, 0, 0), pipeline_mode=u_mode),
            pl.BlockSpec((1, radix, 1, tr, F_WIDTH), lambda b, t: (b, 0, 0, t, 0)),
            pl.BlockSpec((1, radix, 1, tr, F_WIDTH), lambda b, t: (b, 0, 1, tiles - 1 - t, 0)),
        ],
        out_specs=[
            pl.BlockSpec((1, radix, tr, F_WIDTH), lambda b, t: (b, 0, t, 0)),
            pl.BlockSpec((1, radix, tr, F_WIDTH), lambda b, t: (b, 0, tiles - 1 - t, 0)),
        ],
        out_shape=[half_shape, half_shape],
        compiler_params=pltpu.CompilerParams(
            dimension_semantics=("arbitrary", "arbitrary"), vmem_limit_bytes=VMEM_LIMIT),
        name="fourier_mix",
    )(base, cs, chan, rev, tw, u3, z5, z5)
    return (lo.reshape(batch * radix, HALF_DFT, F_WIDTH), hi.reshape(batch * radix, HALF_DFT, F_WIDTH))


def _attention_kernel(sink_ref, q_ref, kp_ref, kc_ref, kn_ref, vp_ref, vc_ref, vn_ref, za_ref,
                      out_ref):
    tq = Q_TILE
    blocks = tq // BLOCK
    i = pl.program_id(1)
    n_blocks = pl.num_programs(1) * blocks
    kcat = jnp.concatenate([kp_ref[0], kc_ref[0], kn_ref[0]], axis=0)
    vcat = jnp.concatenate([vp_ref[0], vc_ref[0], vn_ref[0]], axis=0)
    vcat_t = vcat.astype(F32).T.astype(BF16)
    ones_rows = jnp.ones((ONES_ROWS, 3 * BLOCK), BF16)
    d = (lax.broadcasted_iota(jnp.int32, (BLOCK, BLOCK), 0)
         - lax.broadcasted_iota(jnp.int32, (BLOCK, BLOCK), 1))
    d = jnp.concatenate([d] * GROUP, axis=1)
    units = [(j, g) for j in range(blocks) for g in range(N_KV)]

    def scores(j, g):
        kk = kcat[j * BLOCK:(j + 3) * BLOCK, g * HEAD_DIM:(g + 1) * HEAD_DIM]
        q4 = jnp.concatenate(
            [q_ref[0, j * BLOCK:(j + 1) * BLOCK, h * HEAD_DIM:(h + 1) * HEAD_DIM]
             for h in range(GROUP * g, GROUP * (g + 1))], axis=0)
        return lax.dot_general(kk, q4, (((1,), (1,)), ((), ())), preferred_element_type=F32)

    s_next = scores(*units[0])
    for n, (j, g) in enumerate(units):
        s = s_next
        if n + 1 < len(units):
            s_next = scores(*units[n + 1])
        bj = i * blocks + j
        lo = jnp.where(bj > 0, 0, BLOCK)
        hi = jnp.where(bj < n_blocks - 1, 0, -BLOCK)
        heads = [GROUP * g + hh for hh in range(GROUP)]
        v_t = vcat_t[g * HEAD_DIM:(g + 1) * HEAD_DIM, j * BLOCK:(j + 3) * BLOCK]
        v_aug = jnp.concatenate([v_t, ones_rows], axis=0)
        top = jnp.where(d >= lo, s[:BLOCK], NEG)
        mid = s[BLOCK:2 * BLOCK]
        bot = jnp.where(d <= hi, s[2 * BLOCK:], NEG)
        sk = jnp.concatenate(
            [jnp.full((1, BLOCK), sink_ref[h] * LOG2E, F32) for h in heads], axis=1)
        m = jnp.maximum(jnp.maximum(jnp.max(top, axis=0, keepdims=True),
                                    jnp.max(mid, axis=0, keepdims=True)),
                        jnp.maximum(jnp.max(bot, axis=0, keepdims=True), sk))
        p = jnp.concatenate([jnp.exp2(top - m), jnp.exp2(mid - m), jnp.exp2(bot - m)], axis=0)
        o_aug = jnp.dot(v_aug, p.astype(BF16), preferred_element_type=F32)
        denom = o_aug[HEAD_DIM:HEAD_DIM + 1] + jnp.exp2(sk - m)
        o_t = o_aug[:HEAD_DIM] * (1.0 / denom)
        for hh, h in enumerate(heads):
            o = o_t[:, hh * BLOCK:(hh + 1) * BLOCK].T
            za = za_ref[0, j * BLOCK:(j + 1) * BLOCK, h * HEAD_DIM:(h + 1) * HEAD_DIM]
            out_ref[0, j * BLOCK:(j + 1) * BLOCK, h * HEAD_DIM:(h + 1) * HEAD_DIM] = (
                o * za.astype(F32)).astype(BF16)


def _window_attention(q, k, v, za_silu, sink, batch, seq):
    tq = Q_TILE
    per = tq // BLOCK
    last = seq // BLOCK - 1
    q3 = q.reshape(batch, seq, A_WIDTH)
    k3 = k.reshape(batch, seq, KV_WIDTH)
    v3 = v.reshape(batch, seq, KV_WIDTH)
    z3 = za_silu.reshape(batch, seq, A_WIDTH)
    prev = pl.BlockSpec((1, BLOCK, KV_WIDTH), lambda b, i: (b, jnp.maximum(i * per - 1, 0), 0))
    cur = pl.BlockSpec((1, tq, KV_WIDTH), lambda b, i: (b, i, 0))
    nxt = pl.BlockSpec((1, BLOCK, KV_WIDTH), lambda b, i: (b, jnp.minimum((i + 1) * per, last), 0))
    wide = pl.BlockSpec((1, tq, A_WIDTH), lambda b, i: (b, i, 0))
    out = pl.pallas_call(
        _attention_kernel,
        grid=(batch, seq // tq),
        in_specs=[pl.BlockSpec(memory_space=pltpu.SMEM), wide, prev, cur, nxt, prev, cur, nxt, wide],
        out_specs=wide,
        out_shape=jax.ShapeDtypeStruct((batch, seq, A_WIDTH), BF16),
        compiler_params=pltpu.CompilerParams(
            dimension_semantics=("arbitrary", "arbitrary"), vmem_limit_bytes=VMEM_LIMIT),
        name="window_attention",
    )(sink, q3, k3, k3, k3, v3, v3, v3, z3)
    return out.reshape(batch * seq, A_WIDTH)


def _output_kernel(final, ylo_ref, yhi_ref, ya_ref, g_ref, x_ref, wpf_ref, wpa_ref, wo_ref,
                   fgain_ref, out_ref):
    y_f = jnp.concatenate([ylo_ref[0], yhi_ref[0]], axis=0)
    br_f = jnp.dot(y_f, wpf_ref[...], preferred_element_type=F32)
    br_a = jnp.dot(ya_ref[...], wpa_ref[...], preferred_element_type=F32)
    merged = (g_ref[:, :D_MODEL].astype(F32) * br_f + g_ref[:, D_MODEL:].astype(F32) * br_a)
    y = x_ref[...] + jnp.dot(merged.astype(BF16), wo_ref[...], preferred_element_type=F32)
    if final:
        ms = jnp.mean(y * y, axis=-1, keepdims=True)
        y = (y * lax.rsqrt(ms + EPS)) * fgain_ref[...]
    out_ref[...] = y


def _output_projection(yf_lo, yf_hi, ya, gates, x2, w_pf, w_pa, w_o, final_gain, final):
    tokens = x2.shape[0]
    tm = TOKEN_TILE
    assert tm == SEQ_DFT and yf_lo.shape == (tokens // tm, HALF_DFT, F_WIDTH)
    row = lambda i: (i, 0)
    const2 = lambda i: (0, 0)
    half = pl.BlockSpec((1, HALF_DFT, F_WIDTH), lambda i: (i, 0, 0))
    return pl.pallas_call(
        functools.partial(_output_kernel, final),
        grid=(tokens // tm,),
        in_specs=[
            half,
            half,
            pl.BlockSpec((tm, A_WIDTH), row),
            pl.BlockSpec((tm, GATE_WIDTH), row),
            pl.BlockSpec((tm, D_MODEL), row),
            pl.BlockSpec((F_WIDTH, D_MODEL), const2, pipeline_mode=pl.Buffered(1)),
            pl.BlockSpec((A_WIDTH, D_MODEL), const2, pipeline_mode=pl.Buffered(1)),
            pl.BlockSpec((D_MODEL, D_MODEL), const2, pipeline_mode=pl.Buffered(1)),
            pl.BlockSpec((1, D_MODEL), const2),
        ],
        out_specs=pl.BlockSpec((tm, D_MODEL), row),
        out_shape=jax.ShapeDtypeStruct((tokens, D_MODEL), F32),
        compiler_params=pltpu.CompilerParams(
            dimension_semantics=("arbitrary",), vmem_limit_bytes=VMEM_LIMIT),
        name="output_projection",
    )(yf_lo, yf_hi, ya, gates, x2, w_pf, w_pa, w_o, final_gain.reshape(1, D_MODEL))


def _trunk(x, norm_gain, w_in, gate_bias, sink_logit, w_pf, w_pa, w_o, final_gain):
    batch, seq, _ = x.shape
    depth = norm_gain.shape[0]
    x2 = x.reshape(batch * seq, D_MODEL)
    for l in range(depth):
        uf, zf, q, k, v, za, gates = _in_projection(x2, norm_gain[l], w_in[l], gate_bias[l], seq)
        yf_lo, yf_hi = _fourier_mix(uf, zf, batch, seq)
        ya = _window_attention(q, k, v, za, sink_logit[l], batch, seq)
        x2 = _output_projection(yf_lo, yf_hi, ya, gates, x2, w_pf[l], w_pa[l], w_o[l], final_gain,
                                final=(l == depth - 1))
    return x2.reshape(batch, seq, D_MODEL)


def kernel(x_prompt, x_sample, norm_gain, w_in, gate_bias, sink_logit, w_proj_fourier, w_proj_attn,
           w_out, final_norm_gain):
    w_in_b = w_in.astype(BF16)
    w_pf_b = w_proj_fourier.astype(BF16)
    w_pa_b = w_proj_attn.astype(BF16)
    w_o_b = w_out.astype(BF16)
    args = (norm_gain, w_in_b, gate_bias, sink_logit, w_pf_b, w_pa_b, w_o_b, final_norm_gain)
    return (_trunk(x_prompt, *args), _trunk(x_sample, *args))
```

```python
import functools
import math

import numpy as np
import jax
import jax.numpy as jnp
from jax import lax
from jax.experimental import pallas as pl
from jax.experimental.pallas import tpu as pltpu

F32 = jnp.float32
BF16 = jnp.bfloat16

D_MODEL = 1024
N_FGROUPS = 4
FGROUP = 128
F_WIDTH = N_FGROUPS * FGROUP
HEAD_DIM = 128
N_HEADS = 8
N_KV = 2
GROUP = N_HEADS // N_KV
A_WIDTH = N_HEADS * HEAD_DIM
KV_WIDTH = N_KV * HEAD_DIM
WINDOW = 128
BLOCK = 128
ROT_DIM = HEAD_DIM // 4
ROPE_THETA = 500000.0
EPS = 1e-6
NEG = -1e30
GATE_WIDTH = 2 * D_MODEL
OFF_UF = 0
OFF_ZF = OFF_UF + F_WIDTH
OFF_Q = OFF_ZF + F_WIDTH
OFF_K = OFF_Q + A_WIDTH
OFF_V = OFF_K + KV_WIDTH
OFF_ZA = OFF_V + KV_WIDTH
OFF_G = OFF_ZA + A_WIDTH
IN_WIDTH = OFF_G + GATE_WIDTH

LOG2E = math.log2(math.e)
ONES_ROWS = 16

LANES = 128
SEQ_DFT = 1024
HALF_DFT = SEQ_DFT // 2
DFT_PAD = 16
REV_BLOCK = 128
TOKEN_TILE = 1024
DFT_ROW_TILE = 512
RADIX_ROW_TILE = 128
Q_TILE = 2048
VMEM_LIMIT = 56 * 1024 * 1024


def _silu(x):
    return x * jax.nn.sigmoid(x)


def _rope_tables(seq, tile):
    half = ROT_DIM // 2
    inv = 1.0 / (ROPE_THETA ** (np.arange(half, dtype=np.float64) * 2.0 / ROT_DIM))
    lane_inv = np.zeros((LANES,), np.float64)
    lane_inv[:half] = inv
    lane_inv[half:ROT_DIM] = inv
    rows = np.arange(tile, dtype=np.float64)[:, None] * lane_inv[None, :]
    bases = (np.arange(seq // tile, dtype=np.float64) * tile)[:, None] * lane_inv[None, :]
    bases = np.repeat(bases[:, None, :], 8, axis=1)
    return (jnp.asarray(np.cos(rows), F32), jnp.asarray(np.sin(rows), F32),
            jnp.asarray(np.cos(bases), F32), jnp.asarray(np.sin(bases), F32))


def _rmsnorm_bf16(x, gain):
    ms = jnp.mean(x * x, axis=-1, keepdims=True)
    return ((x * lax.rsqrt(ms + EPS)) * gain).astype(BF16)


def _inproj_kernel(radix, x_ref, gain_ref, w_ref, cr_ref, sr_ref, cb_ref, sb_ref,
                   uf_ref, zf_ref, q_ref, k_ref, v_ref, za_ref, u_scratch):
    h = _rmsnorm_bf16(x_ref[...], gain_ref[0])

    def proj(c0, n):
        return jnp.dot(h, w_ref[0, :, c0:c0 + n], preferred_element_type=F32)

    cb = cb_ref[0, 0:1, :]
    sb = sb_ref[0, 0:1, :]
    cr = cr_ref[...]
    sr = sr_ref[...]
    cos = cb * cr - sb * sr
    sin = sb * cr + cb * sr
    lane = lax.broadcasted_iota(jnp.int32, (1, LANES), 1)
    sin_lo = jnp.where(lane < ROT_DIM // 2, -sin, 0.0)
    sin_hi = jnp.where(lane >= ROT_DIM // 2, sin, 0.0)

    def rope(v):
        return (v * cos + pltpu.roll(v, LANES - ROT_DIM // 2, 1) * sin_lo
                + pltpu.roll(v, ROT_DIM // 2, 1) * sin_hi)

    scale = HEAD_DIM ** -0.5 * LOG2E
    for c in range(A_WIDTH // 512):
        qc = proj(OFF_Q + 512 * c, 512)
        for hh in range(512 // HEAD_DIM):
            lo = hh * HEAD_DIM
            q_ref[:, 512 * c + lo:512 * c + lo + HEAD_DIM] = (
                rope(qc[:, lo:lo + HEAD_DIM]) * scale).astype(BF16)
    for c in range(A_WIDTH // 512):
        za_ref[:, 512 * c:512 * (c + 1)] = _silu(proj(OFF_ZA + 512 * c, 512)).astype(BF16)
    zf_ref[...] = _silu(proj(OFF_ZF, F_WIDTH)).astype(BF16)
    u = proj(OFF_UF, F_WIDTH)
    rows = u.shape[0] // radix
    for g in range(N_FGROUPS):
        u_scratch[g] = u[:, g * FGROUP:(g + 1) * FGROUP]
    for n1 in range(radix):
        for g in range(N_FGROUPS):
            lo = n1 * F_WIDTH + g * FGROUP
            uf_ref[:, lo:lo + FGROUP] = (
                u_scratch[g, pl.ds(n1, rows, stride=radix), :].astype(BF16))
    kv = proj(OFF_K, 2 * KV_WIDTH)
    for hh in range(N_KV):
        lo = hh * HEAD_DIM
        k_ref[:, lo:lo + HEAD_DIM] = rope(kv[:, lo:lo + HEAD_DIM]).astype(BF16)
    v_ref[...] = kv[:, KV_WIDTH:].astype(BF16)


def _in_projection(x2, layer, gains, w_main, seq):
    tokens = x2.shape[0]
    tm = TOKEN_TILE
    tiles_per_seq = seq // tm
    cr, sr, cb, sb = _rope_tables(seq, tm)
    row = lambda i: (i, 0)
    const2 = lambda i: (0, 0)
    radix = seq // SEQ_DFT
    widths = (F_WIDTH, A_WIDTH, KV_WIDTH, KV_WIDTH, A_WIDTH)
    return pl.pallas_call(
        functools.partial(_inproj_kernel, radix),
        grid=(tokens // tm,),
        in_specs=[
            pl.BlockSpec((tm, D_MODEL), row),
            pl.BlockSpec((1, 1, D_MODEL), lambda i: (layer, 0, 0)),
            pl.BlockSpec((1, D_MODEL, OFF_G), lambda i: (layer, 0, 0), pipeline_mode=pl.Buffered(1)),
            pl.BlockSpec((tm, LANES), const2),
            pl.BlockSpec((tm, LANES), const2),
            pl.BlockSpec((1, 8, LANES), lambda i: (i % tiles_per_seq, 0, 0)),
            pl.BlockSpec((1, 8, LANES), lambda i: (i % tiles_per_seq, 0, 0)),
        ],
        out_specs=([pl.BlockSpec((tm // radix, radix * F_WIDTH), row)]
                   + [pl.BlockSpec((tm, w), row) for w in widths]),
        out_shape=([jax.ShapeDtypeStruct((tokens // radix, radix * F_WIDTH), BF16)]
                   + [jax.ShapeDtypeStruct((tokens, w), BF16) for w in widths]),
        scratch_shapes=[pltpu.VMEM((N_FGROUPS, tm, FGROUP), F32)],
        compiler_params=pltpu.CompilerParams(
            dimension_semantics=("arbitrary",), vmem_limit_bytes=VMEM_LIMIT),
        name="in_projection",
    )(x2, gains, w_main, cr, sr, cb, sb)


def _dft_tables(seq, tr):
    n = SEQ_DFT
    radix = seq // n
    tiles = HALF_DFT // tr
    rows = tr + DFT_PAD
    k = np.arange(HALF_DFT + DFT_PAD, dtype=np.int64)
    m = np.arange(n, dtype=np.int64)
    ang = 2.0 * np.pi * ((k[:, None] * m[None, :]) % n).astype(np.float64) / n
    cs = np.stack([np.cos(ang), np.sin(ang)])
    cidx = np.arange(FGROUP, dtype=np.int64)
    cang = 2.0 * np.pi * ((cidx[:, None] * cidx[None, :]) % FGROUP).astype(np.float64) / FGROUP
    cc, sc = np.cos(cang), np.sin(cang)
    norm = 1.0 / math.sqrt(float(seq) * FGROUP)
    chan = np.block([[cc, cc], [sc, -sc]]) * norm
    rev_n = min(REV_BLOCK, tr)
    rev = np.zeros((rev_n, rev_n), np.float64)
    rev[np.arange(1, rev_n), rev_n - np.arange(1, rev_n)] = 1.0
    n1 = np.arange(radix, dtype=np.int64)
    i = np.arange(rows, dtype=np.int64)
    tang = 2.0 * np.pi * (n1[:, None] * i[None, :]).astype(np.float64) / seq
    tw = np.repeat(np.stack([np.cos(tang), np.sin(tang)])[..., None], LANES, axis=-1)
    t0 = np.arange(tiles, dtype=np.int64) * tr
    bang = 2.0 * np.pi * ((t0[:, None] * n1[None, :]) % seq).astype(np.float64) / seq
    base = np.stack([np.cos(bang), np.sin(bang)]).reshape(-1)
    return (jnp.asarray(cs, F32).astype(BF16), jnp.asarray(chan, F32).astype(BF16),
            jnp.asarray(rev, F32).astype(BF16), jnp.asarray(tw, F32), jnp.asarray(base, F32))


def _radix2_fft(re, im):
    n = len(re)
    if n == 1:
        return re, im
    er, ei = _radix2_fft(re[0::2], im[0::2])
    orr, oi = _radix2_fft(re[1::2], im[1::2])
    out_r = [None] * n
    out_i = [None] * n
    for k in range(n // 2):
        ang = -2.0 * math.pi * k / n
        c, s = math.cos(ang), math.sin(ang)
        if k == 0:
            tr, ti = orr[k], oi[k]
        elif 4 * k == n:
            tr, ti = oi[k], -orr[k]
        else:
            tr = orr[k] * c - oi[k] * s
            ti = orr[k] * s + oi[k] * c
        out_r[k] = er[k] + tr
        out_i[k] = ei[k] + ti
        out_r[k + n // 2] = er[k] - tr
        out_i[k + n // 2] = ei[k] - ti
    return out_r, out_i


def _fourier_kernel(radix, tr, tiles, base_ref, cs_ref, chan_ref, rev_ref, tw_ref, u_ref,
                    zlo_ref, zhi_ref, lo_ref, hi_ref):
    t = pl.program_id(1)
    rows = tr + DFT_PAD
    r0 = pl.multiple_of(t * tr, tr)
    c = cs_ref[0, pl.ds(r0, rows), :]
    s = cs_ref[1, pl.ds(r0, rows), :]
    chan = chan_ref[...]
    rev = rev_ref[...]
    rev_n = rev.shape[0]
    pass_groups = N_FGROUPS if radix <= 2 else 2
    width = pass_groups * FGROUP
    first_row = lax.broadcasted_iota(jnp.int32, (8, width), 0) == 0
    for g0 in range(0, N_FGROUPS, pass_groups):
        c0 = g0 * FGROUP
        re, im = [], []
        for n1 in range(radix):
            u = u_ref[0, :, n1 * F_WIDTH + c0:n1 * F_WIDTH + c0 + width]
            p = jnp.dot(c, u, preferred_element_type=F32)
            q = jnp.dot(s, u, preferred_element_type=F32)
            if n1 == 0:
                re.append(p)
                im.append(-q)
                continue
            ct, st = tw_ref[0, n1], tw_ref[1, n1]
            if tiles > 1:
                cb = base_ref[t * radix + n1]
                sb = base_ref[(tiles + t) * radix + n1]
                ct, st = cb * ct - sb * st, sb * ct + cb * st
            ct = jnp.concatenate([ct] * pass_groups, axis=1)
            st = jnp.concatenate([st] * pass_groups, axis=1)
            re.append(p * ct - q * st)
            im.append(-(p * st + q * ct))
        xr, xi = _radix2_fft(re, im)
        for k1 in range(radix):
            xrb = xr[k1].astype(BF16)
            xib = xi[k1].astype(BF16)
            y_lo, y_hi = [], []
            for g in range(pass_groups):
                lo = g * FGROUP
                lhs = jnp.concatenate([xrb[:, lo:lo + FGROUP], xib[:, lo:lo + FGROUP]], axis=1)
                z = jnp.dot(lhs, chan, preferred_element_type=F32)
                y_lo.append(z[:tr, :FGROUP])
                y_hi.append(z[:, FGROUP:])
            y_lo = jnp.concatenate(y_lo, axis=1)
            lo_ref[0, k1, :, c0:c0 + width] = (
                y_lo * zlo_ref[0, k1, 0, :, c0:c0 + width].astype(F32)).astype(BF16)
            y_hi = jnp.concatenate(y_hi, axis=1).astype(BF16)
            km = radix - 1 - k1
            for b in range(tr // rev_n):
                top = tr - rev_n * b
                blk = jnp.dot(rev, y_hi[top - rev_n:top], preferred_element_type=F32)
                head = jnp.where(first_row, y_hi[top:top + 16].astype(F32)[:8], blk[:8])
                blk = jnp.concatenate([head, blk[8:]], axis=0)
                z_hi = zhi_ref[0, km, 0, rev_n * b:rev_n * (b + 1), c0:c0 + width].astype(F32)
                hi_ref[0, km, rev_n * b:rev_n * (b + 1), c0:c0 + width] = (blk * z_hi).astype(BF16)


def _fourier_mix(u_il, zf_silu, batch, seq):
    n = SEQ_DFT
    radix = seq // n
    tr = DFT_ROW_TILE if radix <= 2 else RADIX_ROW_TILE
    tiles = HALF_DFT // tr
    cs, chan, rev, tw, base = _dft_tables(seq, tr)
    u3 = u_il.reshape(batch, n, radix * F_WIDTH)
    z5 = zf_silu.reshape(batch, radix, 2, HALF_DFT, F_WIDTH)
    u_mode = pl.Buffered(1) if batch == 1 else pl.Buffered(2)
    const = lambda nd: (lambda b, t: (0,) * nd)
    half_shape = jax.ShapeDtypeStruct((batch, radix, HALF_DFT, F_WIDTH), BF16)
    lo, hi = pl.pallas_call(
        functools.partial(_fourier_kernel, radix, tr, tiles),
        grid=(batch, tiles),
        in_specs=[
            pl.BlockSpec(memory_space=pltpu.SMEM),
            pl.BlockSpec(cs.shape, const(3), pipeline_mode=pl.Buffered(1)),
            pl.BlockSpec(chan.shape, const(2)),
            pl.BlockSpec(rev.shape, const(2)),
            pl.BlockSpec(tw.shape, const(4), pipeline_mode=pl.Buffered(1)),
            pl.BlockSpec((1, n, radix * F_WIDTH), lambda b, t: (b, 0, 0), pipeline_mode=u_mode),
            pl.BlockSpec((1, radix, 1, tr, F_WIDTH), lambda b, t: (b, 0, 0, t, 0)),
            pl.BlockSpec((1, radix, 1, tr, F_WIDTH), lambda b, t: (b, 0, 1, tiles - 1 - t, 0)),
        ],
        out_specs=[
            pl.BlockSpec((1, radix, tr, F_WIDTH), lambda b, t: (b, 0, t, 0)),
            pl.BlockSpec((1, radix, tr, F_WIDTH), lambda b, t: (b, 0, tiles - 1 - t, 0)),
        ],
        out_shape=[half_shape, half_shape],
        compiler_params=pltpu.CompilerParams(
            dimension_semantics=("arbitrary", "arbitrary"), vmem_limit_bytes=VMEM_LIMIT),
        name="fourier_mix",
    )(base, cs, chan, rev, tw, u3, z5, z5)
    return (lo.reshape(batch * radix, HALF_DFT, F_WIDTH), hi.reshape(batch * radix, HALF_DFT, F_WIDTH))


def _attention_kernel(layer, sink_ref, q_ref, kp_ref, kc_ref, kn_ref, vp_ref, vc_ref, vn_ref, za_ref,
                      out_ref):
    tq = Q_TILE
    blocks = tq // BLOCK
    i = pl.program_id(1)
    n_blocks = pl.num_programs(1) * blocks
    kcat = jnp.concatenate([kp_ref[0], kc_ref[0], kn_ref[0]], axis=0)
    vcat = jnp.concatenate([vp_ref[0], vc_ref[0], vn_ref[0]], axis=0)
    vcat_t = vcat.astype(F32).T.astype(BF16)
    ones_rows = jnp.ones((ONES_ROWS, 3 * BLOCK), BF16)
    d = (lax.broadcasted_iota(jnp.int32, (BLOCK, BLOCK), 0)
         - lax.broadcasted_iota(jnp.int32, (BLOCK, BLOCK), 1))
    d = jnp.concatenate([d] * N_HEADS, axis=1)

    def scores(j):
        cols = []
        for g in range(N_KV):
            kk = kcat[j * BLOCK:(j + 3) * BLOCK, g * HEAD_DIM:(g + 1) * HEAD_DIM]
            q4 = jnp.concatenate(
                [q_ref[0, j * BLOCK:(j + 1) * BLOCK, h * HEAD_DIM:(h + 1) * HEAD_DIM]
                 for h in range(GROUP * g, GROUP * (g + 1))], axis=0)
            cols.append(lax.dot_general(kk, q4, (((1,), (1,)), ((), ())), preferred_element_type=F32))
        return jnp.concatenate(cols, axis=1)

    s_next = scores(0)
    for j in range(blocks):
        s = s_next
        if j + 1 < blocks:
            s_next = scores(j + 1)
        bj = i * blocks + j
        lo = jnp.where(bj > 0, 0, BLOCK)
        hi = jnp.where(bj < n_blocks - 1, 0, -BLOCK)
        top = jnp.where(d >= lo, s[:BLOCK], NEG)
        mid = s[BLOCK:2 * BLOCK]
        bot = jnp.where(d <= hi, s[2 * BLOCK:], NEG)
        sk = jnp.concatenate(
            [jnp.full((1, BLOCK), sink_ref[layer * N_HEADS + h] * LOG2E, F32) for h in range(N_HEADS)],
            axis=1)
        m = jnp.maximum(jnp.maximum(jnp.max(top, axis=0, keepdims=True),
                                    jnp.max(mid, axis=0, keepdims=True)),
                        jnp.maximum(jnp.max(bot, axis=0, keepdims=True), sk))
        p = jnp.concatenate([jnp.exp2(top - m), jnp.exp2(mid - m), jnp.exp2(bot - m)],
                            axis=0).astype(BF16)
        p_sink = jnp.exp2(sk - m)
        for g in range(N_KV):
            v_t = vcat_t[g * HEAD_DIM:(g + 1) * HEAD_DIM, j * BLOCK:(j + 3) * BLOCK]
            v_aug = jnp.concatenate([v_t, ones_rows], axis=0)
            w = GROUP * BLOCK
            o_aug = jnp.dot(v_aug, p[:, g * w:(g + 1) * w], preferred_element_type=F32)
            denom = o_aug[HEAD_DIM:HEAD_DIM + 1] + p_sink[:, g * w:(g + 1) * w]
            o_t = o_aug[:HEAD_DIM] * (1.0 / denom)
            for hh in range(GROUP):
                h = GROUP * g + hh
                o = o_t[:, hh * BLOCK:(hh + 1) * BLOCK].T
                za = za_ref[0, j * BLOCK:(j + 1) * BLOCK, h * HEAD_DIM:(h + 1) * HEAD_DIM]
                out_ref[0, j * BLOCK:(j + 1) * BLOCK, h * HEAD_DIM:(h + 1) * HEAD_DIM] = (
                    o * za.astype(F32)).astype(BF16)


def _window_attention(q, k, v, za_silu, layer, sinks, batch, seq):
    tq = Q_TILE
    per = tq // BLOCK
    last = seq // BLOCK - 1
    q3 = q.reshape(batch, seq, A_WIDTH)
    k3 = k.reshape(batch, seq, KV_WIDTH)
    v3 = v.reshape(batch, seq, KV_WIDTH)
    z3 = za_silu.reshape(batch, seq, A_WIDTH)
    prev = pl.BlockSpec((1, BLOCK, KV_WIDTH), lambda b, i: (b, jnp.maximum(i * per - 1, 0), 0))
    cur = pl.BlockSpec((1, tq, KV_WIDTH), lambda b, i: (b, i, 0))
    nxt = pl.BlockSpec((1, BLOCK, KV_WIDTH), lambda b, i: (b, jnp.minimum((i + 1) * per, last), 0))
    wide = pl.BlockSpec((1, tq, A_WIDTH), lambda b, i: (b, i, 0))
    out = pl.pallas_call(
        functools.partial(_attention_kernel, layer),
        grid=(batch, seq // tq),
        in_specs=[pl.BlockSpec(memory_space=pltpu.SMEM), wide, prev, cur, nxt, prev, cur, nxt, wide],
        out_specs=wide,
        out_shape=jax.ShapeDtypeStruct((batch, seq, A_WIDTH), BF16),
        compiler_params=pltpu.CompilerParams(
            dimension_semantics=("arbitrary", "arbitrary"), vmem_limit_bytes=VMEM_LIMIT),
        name="window_attention",
    )(sinks, q3, k3, k3, k3, v3, v3, v3, z3)
    return out.reshape(batch * seq, A_WIDTH)


def _output_kernel(final, ylo_ref, yhi_ref, ya_ref, x_ref, gain_ref, wg_ref, bias_ref, wpf_ref,
                   wpa_ref, wo_ref, fgain_ref, out_ref):
    x = x_ref[...]
    h = _rmsnorm_bf16(x, gain_ref[0])
    y_f = jnp.concatenate([ylo_ref[0], yhi_ref[0]], axis=0)
    br_f = jnp.dot(y_f, wpf_ref[0], preferred_element_type=F32)
    br_a = jnp.dot(ya_ref[...], wpa_ref[0], preferred_element_type=F32)

    def gate(c0):
        pre = jnp.dot(h, wg_ref[0, :, c0:c0 + 512], preferred_element_type=F32)
        return jax.nn.sigmoid(pre + bias_ref[0, :, c0:c0 + 512])

    merged = []
    for c in range(D_MODEL // 512):
        lo = 512 * c
        merged.append((gate(lo) * br_f[:, lo:lo + 512]
                       + gate(D_MODEL + lo) * br_a[:, lo:lo + 512]).astype(BF16))
    merged = jnp.concatenate(merged, axis=1)
    y = x + jnp.dot(merged, wo_ref[0], preferred_element_type=F32)
    if final:
        ms = jnp.mean(y * y, axis=-1, keepdims=True)
        y = (y * lax.rsqrt(ms + EPS)) * fgain_ref[...]
    out_ref[...] = y


def _output_projection(yf_lo, yf_hi, ya, x2, layer, gains, w_gate, bias, w_pf, w_pa, w_o,
                       final_gain, final):
    tokens = x2.shape[0]
    tm = TOKEN_TILE
    assert tm == SEQ_DFT and yf_lo.shape == (tokens // tm, HALF_DFT, F_WIDTH)
    row = lambda i: (i, 0)
    const2 = lambda i: (0, 0)
    of_layer = lambda i: (layer, 0, 0)
    half = pl.BlockSpec((1, HALF_DFT, F_WIDTH), lambda i: (i, 0, 0))
    return pl.pallas_call(
        functools.partial(_output_kernel, final),
        grid=(tokens // tm,),
        in_specs=[
            half,
            half,
            pl.BlockSpec((tm, A_WIDTH), row),
            pl.BlockSpec((tm, D_MODEL), row),
            pl.BlockSpec((1, 1, D_MODEL), of_layer),
            pl.BlockSpec((1, D_MODEL, GATE_WIDTH), of_layer, pipeline_mode=pl.Buffered(1)),
            pl.BlockSpec((1, 1, GATE_WIDTH), of_layer),
            pl.BlockSpec((1, F_WIDTH, D_MODEL), of_layer, pipeline_mode=pl.Buffered(1)),
            pl.BlockSpec((1, A_WIDTH, D_MODEL), of_layer, pipeline_mode=pl.Buffered(1)),
            pl.BlockSpec((1, D_MODEL, D_MODEL), of_layer, pipeline_mode=pl.Buffered(1)),
            pl.BlockSpec((1, D_MODEL), const2),
        ],
        out_specs=pl.BlockSpec((tm, D_MODEL), row),
        out_shape=jax.ShapeDtypeStruct((tokens, D_MODEL), F32),
        compiler_params=pltpu.CompilerParams(
            dimension_semantics=("arbitrary",), vmem_limit_bytes=VMEM_LIMIT),
        name="output_projection",
    )(yf_lo, yf_hi, ya, x2, gains, w_gate, bias, w_pf, w_pa, w_o, final_gain.reshape(1, D_MODEL))


def _trunk(x, gains, w_main, w_gate, bias, sinks, w_pf, w_pa, w_o, final_gain):
    batch, seq, _ = x.shape
    depth = gains.shape[0]
    x2 = x.reshape(batch * seq, D_MODEL)
    for l in range(depth):
        uf, zf, q, k, v, za = _in_projection(x2, l, gains, w_main, seq)
        yf_lo, yf_hi = _fourier_mix(uf, zf, batch, seq)
        ya = _window_attention(q, k, v, za, l, sinks, batch, seq)
        x2 = _output_projection(yf_lo, yf_hi, ya, x2, l, gains, w_gate, bias, w_pf, w_pa, w_o,
                                final_gain, final=(l == depth - 1))
    return x2.reshape(batch, seq, D_MODEL)


def _prepare(norm_gain, w_in, gate_bias, sink_logit, w_proj_fourier, w_proj_attn, w_out):
    depth = norm_gain.shape[0]
    return (norm_gain.reshape(depth, 1, D_MODEL),
            w_in[:, :, :OFF_G].astype(BF16),
            w_in[:, :, OFF_G:].astype(BF16),
            gate_bias.reshape(depth, 1, GATE_WIDTH),
            sink_logit.reshape(depth * N_HEADS),
            w_proj_fourier.astype(BF16), w_proj_attn.astype(BF16), w_out.astype(BF16))


def kernel(x_prompt, x_sample, norm_gain, w_in, gate_bias, sink_logit, w_proj_fourier, w_proj_attn,
           w_out, final_norm_gain):
    params = _prepare(norm_gain, w_in, gate_bias, sink_logit, w_proj_fourier, w_proj_attn, w_out)
    return (_trunk(x_prompt, *params, final_norm_gain), _trunk(x_sample, *params, final_norm_gain))
```

```python
import functools
import math

import numpy as np
import jax
import jax.numpy as jnp
from jax import lax
from jax.experimental import pallas as pl
from jax.experimental.pallas import tpu as pltpu

F32 = jnp.float32
BF16 = jnp.bfloat16

D_MODEL = 1024
N_FGROUPS = 4
FGROUP = 128
F_WIDTH = N_FGROUPS * FGROUP
HEAD_DIM = 128
N_HEADS = 8
N_KV = 2
GROUP = N_HEADS // N_KV
A_WIDTH = N_HEADS * HEAD_DIM
KV_WIDTH = N_KV * HEAD_DIM
WINDOW = 128
BLOCK = 128
ROT_DIM = HEAD_DIM // 4
ROPE_THETA = 500000.0
EPS = 1e-6
NEG = -1e30
GATE_WIDTH = 2 * D_MODEL
OFF_UF = 0
OFF_ZF = OFF_UF + F_WIDTH
OFF_Q = OFF_ZF + F_WIDTH
OFF_K = OFF_Q + A_WIDTH
OFF_V = OFF_K + KV_WIDTH
OFF_ZA = OFF_V + KV_WIDTH
OFF_G = OFF_ZA + A_WIDTH
IN_WIDTH = OFF_G + GATE_WIDTH
GATE_BLOCK = IN_WIDTH // 2
GATE_SKIP = OFF_G - GATE_BLOCK

LOG2E = math.log2(math.e)
ONES_ROWS = 16

LANES = 128
SEQ_DFT = 1024
HALF_DFT = SEQ_DFT // 2
DFT_PAD = 16
REV_BLOCK = 128
TOKEN_TILE = 1024
DFT_ROW_TILE = 512
RADIX_ROW_TILE = 128
VMEM_LIMIT = 56 * 1024 * 1024


def _silu(x):
    return x * jax.nn.sigmoid(x)


def _rope_tables(seq, tile):
    half = ROT_DIM // 2
    inv = 1.0 / (ROPE_THETA ** (np.arange(half, dtype=np.float64) * 2.0 / ROT_DIM))
    lane_inv = np.zeros((LANES,), np.float64)
    lane_inv[:half] = inv
    lane_inv[half:ROT_DIM] = inv
    rows = np.arange(tile, dtype=np.float64)[:, None] * lane_inv[None, :]
    bases = (np.arange(seq // tile, dtype=np.float64) * tile)[:, None] * lane_inv[None, :]
    bases = np.repeat(bases[:, None, :], 8, axis=1)
    return (jnp.asarray(np.cos(rows), F32), jnp.asarray(np.sin(rows), F32),
            jnp.asarray(np.cos(bases), F32), jnp.asarray(np.sin(bases), F32))


def _rmsnorm_bf16(x, gain):
    ms = jnp.mean(x * x, axis=-1, keepdims=True)
    return ((x * lax.rsqrt(ms + EPS)) * gain).astype(BF16)


def _inproj_kernel(radix, x_ref, gain_ref, w_ref, cr_ref, sr_ref, cb_ref, sb_ref,
                   uf_ref, zf_ref, q_ref, k_ref, v_ref, za_ref, u_scratch):
    h = _rmsnorm_bf16(x_ref[...], gain_ref[0])

    def proj(c0, n):
        return jnp.dot(h, w_ref[0, :, c0:c0 + n], preferred_element_type=F32)

    cb = cb_ref[0, 0:1, :]
    sb = sb_ref[0, 0:1, :]
    cr = cr_ref[...]
    sr = sr_ref[...]
    cos = cb * cr - sb * sr
    sin = sb * cr + cb * sr
    lane = lax.broadcasted_iota(jnp.int32, (1, LANES), 1)
    sin_lo = jnp.where(lane < ROT_DIM // 2, -sin, 0.0)
    sin_hi = jnp.where(lane >= ROT_DIM // 2, sin, 0.0)

    def rope(v):
        return (v * cos + pltpu.roll(v, LANES - ROT_DIM // 2, 1) * sin_lo
                + pltpu.roll(v, ROT_DIM // 2, 1) * sin_hi)

    scale = HEAD_DIM ** -0.5 * LOG2E
    for c in range(A_WIDTH // 512):
        qc = proj(OFF_Q + 512 * c, 512)
        for hh in range(512 // HEAD_DIM):
            lo = hh * HEAD_DIM
            q_ref[:, 512 * c + lo:512 * c + lo + HEAD_DIM] = (
                rope(qc[:, lo:lo + HEAD_DIM]) * scale).astype(BF16)
    for c in range(A_WIDTH // 512):
        za_ref[:, 512 * c:512 * (c + 1)] = _silu(proj(OFF_ZA + 512 * c, 512)).astype(BF16)
    zf_ref[...] = _silu(proj(OFF_ZF, F_WIDTH)).astype(BF16)
    u = proj(OFF_UF, F_WIDTH)
    rows = u.shape[0] // radix
    for g in range(N_FGROUPS):
        u_scratch[g] = u[:, g * FGROUP:(g + 1) * FGROUP]
    for n1 in range(radix):
        for g in range(N_FGROUPS):
            lo = n1 * F_WIDTH + g * FGROUP
            uf_ref[:, lo:lo + FGROUP] = (
                u_scratch[g, pl.ds(n1, rows, stride=radix), :].astype(BF16))
    kv = proj(OFF_K, 2 * KV_WIDTH)
    for hh in range(N_KV):
        lo = hh * HEAD_DIM
        k_ref[:, lo:lo + HEAD_DIM] = rope(kv[:, lo:lo + HEAD_DIM]).astype(BF16)
    v_ref[...] = kv[:, KV_WIDTH:].astype(BF16)


def _in_projection(x2, layer, gains, w_in, seq):
    tokens = x2.shape[0]
    tm = TOKEN_TILE
    tiles_per_seq = seq // tm
    cr, sr, cb, sb = _rope_tables(seq, tm)
    row = lambda i: (i, 0)
    const2 = lambda i: (0, 0)
    radix = seq // SEQ_DFT
    widths = (F_WIDTH, A_WIDTH, KV_WIDTH, KV_WIDTH, A_WIDTH)
    return pl.pallas_call(
        functools.partial(_inproj_kernel, radix),
        grid=(tokens // tm,),
        in_specs=[
            pl.BlockSpec((tm, D_MODEL), row),
            pl.BlockSpec((1, 1, D_MODEL), lambda i: (layer, 0, 0)),
            pl.BlockSpec((1, D_MODEL, OFF_G), lambda i: (layer, 0, 0), pipeline_mode=pl.Buffered(1)),
            pl.BlockSpec((tm, LANES), const2),
            pl.BlockSpec((tm, LANES), const2),
            pl.BlockSpec((1, 8, LANES), lambda i: (i % tiles_per_seq, 0, 0)),
            pl.BlockSpec((1, 8, LANES), lambda i: (i % tiles_per_seq, 0, 0)),
        ],
        out_specs=([pl.BlockSpec((tm // radix, radix * F_WIDTH), row)]
                   + [pl.BlockSpec((tm, w), row) for w in widths]),
        out_shape=([jax.ShapeDtypeStruct((tokens // radix, radix * F_WIDTH), BF16)]
                   + [jax.ShapeDtypeStruct((tokens, w), BF16) for w in widths]),
        scratch_shapes=[pltpu.VMEM((N_FGROUPS, tm, FGROUP), F32)],
        compiler_params=pltpu.CompilerParams(
            dimension_semantics=("arbitrary",), vmem_limit_bytes=VMEM_LIMIT),
        name="in_projection",
    )(x2, gains, w_in, cr, sr, cb, sb)


def _dft_tables(seq, tr):
    n = SEQ_DFT
    radix = seq // n
    tiles = HALF_DFT // tr
    rows = tr + DFT_PAD
    k = np.arange(HALF_DFT + DFT_PAD, dtype=np.int64)
    m = np.arange(n, dtype=np.int64)
    ang = 2.0 * np.pi * ((k[:, None] * m[None, :]) % n).astype(np.float64) / n
    cs = np.stack([np.cos(ang), np.sin(ang)])
    cidx = np.arange(FGROUP, dtype=np.int64)
    cang = 2.0 * np.pi * ((cidx[:, None] * cidx[None, :]) % FGROUP).astype(np.float64) / FGROUP
    cc, sc = np.cos(cang), np.sin(cang)
    norm = 1.0 / math.sqrt(float(seq) * FGROUP)
    chan = np.block([[cc, cc], [sc, -sc]]) * norm
    rev_n = min(REV_BLOCK, tr)
    rev = np.zeros((rev_n, rev_n), np.float64)
    rev[np.arange(1, rev_n), rev_n - np.arange(1, rev_n)] = 1.0
    n1 = np.arange(radix, dtype=np.int64)
    i = np.arange(rows, dtype=np.int64)
    tang = 2.0 * np.pi * (n1[:, None] * i[None, :]).astype(np.float64) / seq
    tw = np.repeat(np.stack([np.cos(tang), np.sin(tang)])[..., None], LANES, axis=-1)
    t0 = np.arange(tiles, dtype=np.int64) * tr
    bang = 2.0 * np.pi * ((t0[:, None] * n1[None, :]) % seq).astype(np.float64) / seq
    base = np.stack([np.cos(bang), np.sin(bang)]).reshape(-1)
    return (jnp.asarray(cs, F32).astype(BF16), jnp.asarray(chan, F32).astype(BF16),
            jnp.asarray(rev, F32).astype(BF16), jnp.asarray(tw, F32), jnp.asarray(base, F32))


def _radix2_fft(re, im):
    n = len(re)
    if n == 1:
        return re, im
    er, ei = _radix2_fft(re[0::2], im[0::2])
    orr, oi = _radix2_fft(re[1::2], im[1::2])
    out_r = [None] * n
    out_i = [None] * n
    for k in range(n // 2):
        ang = -2.0 * math.pi * k / n
        c, s = math.cos(ang), math.sin(ang)
        if k == 0:
            tr, ti = orr[k], oi[k]
        elif 4 * k == n:
            tr, ti = oi[k], -orr[k]
        else:
            tr = orr[k] * c - oi[k] * s
            ti = orr[k] * s + oi[k] * c
        out_r[k] = er[k] + tr
        out_i[k] = ei[k] + ti
        out_r[k + n // 2] = er[k] - tr
        out_i[k + n // 2] = ei[k] - ti
    return out_r, out_i


def _fourier_kernel(radix, tr, tiles, base_ref, cs_ref, chan_ref, rev_ref, tw_ref, u_ref,
                    zlo_ref, zhi_ref, lo_ref, hi_ref):
    t = pl.program_id(1)
    rows = tr + DFT_PAD
    r0 = pl.multiple_of(t * tr, tr)
    c = cs_ref[0, pl.ds(r0, rows), :]
    s = cs_ref[1, pl.ds(r0, rows), :]
    chan = chan_ref[...]
    rev = rev_ref[...]
    rev_n = rev.shape[0]
    pass_groups = N_FGROUPS if radix <= 2 else 2
    width = pass_groups * FGROUP
    first_row = lax.broadcasted_iota(jnp.int32, (8, width), 0) == 0
    for g0 in range(0, N_FGROUPS, pass_groups):
        c0 = g0 * FGROUP
        re, im = [], []
        for n1 in range(radix):
            u = u_ref[0, :, n1 * F_WIDTH + c0:n1 * F_WIDTH + c0 + width]
            p = jnp.dot(c, u, preferred_element_type=F32)
            q = jnp.dot(s, u, preferred_element_type=F32)
            if n1 == 0:
                re.append(p)
                im.append(-q)
                continue
            ct, st = tw_ref[0, n1], tw_ref[1, n1]
            if tiles > 1:
                cb = base_ref[t * radix + n1]
                sb = base_ref[(tiles + t) * radix + n1]
                ct, st = cb * ct - sb * st, sb * ct + cb * st
            ct = jnp.concatenate([ct] * pass_groups, axis=1)
            st = jnp.concatenate([st] * pass_groups, axis=1)
            re.append(p * ct - q * st)
            im.append(-(p * st + q * ct))
        xr, xi = _radix2_fft(re, im)
        for k1 in range(radix):
            xrb = xr[k1].astype(BF16)
            xib = xi[k1].astype(BF16)
            y_lo, y_hi = [], []
            for g in range(pass_groups):
                lo = g * FGROUP
                lhs = jnp.concatenate([xrb[:, lo:lo + FGROUP], xib[:, lo:lo + FGROUP]], axis=1)
                z = jnp.dot(lhs, chan, preferred_element_type=F32)
                y_lo.append(z[:tr, :FGROUP])
                y_hi.append(z[:, FGROUP:])
            y_lo = jnp.concatenate(y_lo, axis=1)
            lo_ref[0, k1, :, c0:c0 + width] = (
                y_lo * zlo_ref[0, k1, 0, :, c0:c0 + width].astype(F32)).astype(BF16)
            y_hi = jnp.concatenate(y_hi, axis=1).astype(BF16)
            km = radix - 1 - k1
            for b in range(tr // rev_n):
                top = tr - rev_n * b
                blk = jnp.dot(rev, y_hi[top - rev_n:top], preferred_element_type=F32)
                head = jnp.where(first_row, y_hi[top:top + 16].astype(F32)[:8], blk[:8])
                blk = jnp.concatenate([head, blk[8:]], axis=0)
                z_hi = zhi_ref[0, km, 0, rev_n * b:rev_n * (b + 1), c0:c0 + width].astype(F32)
                hi_ref[0, km, rev_n * b:rev_n * (b + 1), c0:c0 + width] = (blk * z_hi).astype(BF16)


def _fourier_mix(u_il, zf_silu, batch, seq):
    n = SEQ_DFT
    radix = seq // n
    tr = DFT_ROW_TILE if radix <= 2 else RADIX_ROW_TILE
    tiles = HALF_DFT // tr
    cs, chan, rev, tw, base = _dft_tables(seq, tr)
    u3 = u_il.reshape(batch, n, radix * F_WIDTH)
    z5 = zf_silu.reshape(batch, radix, 2, HALF_DFT, F_WIDTH)
    u_mode = pl.Buffered(1) if batch == 1 else pl.Buffered(2)
    const = lambda nd: (lambda b, t: (0,) * nd)
    half_shape = jax.ShapeDtypeStruct((batch, radix, HALF_DFT, F_WIDTH), BF16)
    lo, hi = pl.pallas_call(
        functools.partial(_fourier_kernel, radix, tr, tiles),
        grid=(batch, tiles),
        in_specs=[
            pl.BlockSpec(memory_space=pltpu.SMEM),
            pl.BlockSpec(cs.shape, const(3), pipeline_mode=pl.Buffered(1)),
            pl.BlockSpec(chan.shape, const(2)),
            pl.BlockSpec(rev.shape, const(2)),
            pl.BlockSpec(tw.shape, const(4), pipeline_mode=pl.Buffered(1)),
            pl.BlockSpec((1, n, radix * F_WIDTH), lambda b, t: (b, 0, 0), pipeline_mode=u_mode),
            pl.BlockSpec((1, radix, 1, tr, F_WIDTH), lambda b, t: (b, 0, 0, t, 0)),
            pl.BlockSpec((1, radix, 1, tr, F_WIDTH), lambda b, t: (b, 0, 1, tiles - 1 - t, 0)),
        ],
        out_specs=[
            pl.BlockSpec((1, radix, tr, F_WIDTH), lambda b, t: (b, 0, t, 0)),
            pl.BlockSpec((1, radix, tr, F_WIDTH), lambda b, t: (b, 0, tiles - 1 - t, 0)),
        ],
        out_shape=[half_shape, half_shape],
        compiler_params=pltpu.CompilerParams(
            dimension_semantics=("arbitrary", "arbitrary"), vmem_limit_bytes=VMEM_LIMIT),
        name="fourier_mix",
    )(base, cs, chan, rev, tw, u3, z5, z5)
    return (lo.reshape(batch * radix, HALF_DFT, F_WIDTH), hi.reshape(batch * radix, HALF_DFT, F_WIDTH))


def _attn_out_kernel(final, layer, n_blocks, sink_ref, q_ref, kp_ref, kc_ref, kn_ref, vp_ref, vc_ref,
                     vn_ref, za_ref, ylo_ref, yhi_ref, x_ref, gain_ref, wg_ref, bias_ref, wpf_ref,
                     wpa_ref, wo_ref, fgain_ref, out_ref, ya_scratch, g_scratch):
    tm = TOKEN_TILE
    blocks = tm // BLOCK
    tiles_per_seq = n_blocks // blocks
    ti = pl.program_id(0) % tiles_per_seq
    x = x_ref[...]
    h = _rmsnorm_bf16(x, gain_ref[0])
    kcat = jnp.concatenate([kp_ref[...], kc_ref[...], kn_ref[...]], axis=0)
    vcat = jnp.concatenate([vp_ref[...], vc_ref[...], vn_ref[...]], axis=0)
    vcat_t = vcat.astype(F32).T.astype(BF16)
    ones_rows = jnp.ones((ONES_ROWS, 3 * BLOCK), BF16)
    d = (lax.broadcasted_iota(jnp.int32, (BLOCK, BLOCK), 0)
         - lax.broadcasted_iota(jnp.int32, (BLOCK, BLOCK), 1))
    d = jnp.concatenate([d] * N_HEADS, axis=1)
    gate_chunk = GATE_WIDTH // blocks

    def scores(j):
        cols = []
        for g in range(N_KV):
            kk = kcat[j * BLOCK:(j + 3) * BLOCK, g * HEAD_DIM:(g + 1) * HEAD_DIM]
            q4 = jnp.concatenate(
                [q_ref[j * BLOCK:(j + 1) * BLOCK, hd * HEAD_DIM:(hd + 1) * HEAD_DIM]
                 for hd in range(GROUP * g, GROUP * (g + 1))], axis=0)
            cols.append(lax.dot_general(kk, q4, (((1,), (1,)), ((), ())), preferred_element_type=F32))
        return jnp.concatenate(cols, axis=1)

    s_next = scores(0)
    for j in range(blocks):
        s = s_next
        if j + 1 < blocks:
            s_next = scores(j + 1)
        c0 = j * gate_chunk
        pre = jnp.dot(h, wg_ref[0, :, GATE_SKIP + c0:GATE_SKIP + c0 + gate_chunk],
                      preferred_element_type=F32)
        g_scratch[:, c0:c0 + gate_chunk] = jax.nn.sigmoid(
            pre + bias_ref[0, :, c0:c0 + gate_chunk]).astype(BF16)
        bj = ti * blocks + j
        lo = jnp.where(bj > 0, 0, BLOCK)
        hi = jnp.where(bj < n_blocks - 1, 0, -BLOCK)
        top = jnp.where(d >= lo, s[:BLOCK], NEG)
        mid = s[BLOCK:2 * BLOCK]
        bot = jnp.where(d <= hi, s[2 * BLOCK:], NEG)
        sk = jnp.concatenate(
            [jnp.full((1, BLOCK), sink_ref[layer * N_HEADS + hd] * LOG2E, F32) for hd in range(N_HEADS)],
            axis=1)
        m = jnp.maximum(jnp.maximum(jnp.max(top, axis=0, keepdims=True),
                                    jnp.max(mid, axis=0, keepdims=True)),
                        jnp.maximum(jnp.max(bot, axis=0, keepdims=True), sk))
        p = jnp.concatenate([jnp.exp2(top - m), jnp.exp2(mid - m), jnp.exp2(bot - m)],
                            axis=0).astype(BF16)
        p_sink = jnp.exp2(sk - m)
        for g in range(N_KV):
            v_t = vcat_t[g * HEAD_DIM:(g + 1) * HEAD_DIM, j * BLOCK:(j + 3) * BLOCK]
            v_aug = jnp.concatenate([v_t, ones_rows], axis=0)
            w = GROUP * BLOCK
            o_aug = jnp.dot(v_aug, p[:, g * w:(g + 1) * w], preferred_element_type=F32)
            denom = o_aug[HEAD_DIM:HEAD_DIM + 1] + p_sink[:, g * w:(g + 1) * w]
            o_t = o_aug[:HEAD_DIM] * (1.0 / denom)
            for hh in range(GROUP):
                hd = GROUP * g + hh
                o = o_t[:, hh * BLOCK:(hh + 1) * BLOCK].T
                za = za_ref[j * BLOCK:(j + 1) * BLOCK, hd * HEAD_DIM:(hd + 1) * HEAD_DIM]
                ya_scratch[j * BLOCK:(j + 1) * BLOCK, hd * HEAD_DIM:(hd + 1) * HEAD_DIM] = (
                    o * za.astype(F32)).astype(BF16)

    y_f = jnp.concatenate([ylo_ref[0], yhi_ref[0]], axis=0)
    br_f = jnp.dot(y_f, wpf_ref[0], preferred_element_type=F32)
    br_a = jnp.dot(ya_scratch[...], wpa_ref[0], preferred_element_type=F32)
    merged = (g_scratch[:, :D_MODEL].astype(F32) * br_f
              + g_scratch[:, D_MODEL:].astype(F32) * br_a).astype(BF16)
    y = x + jnp.dot(merged, wo_ref[0], preferred_element_type=F32)
    if final:
        ms = jnp.mean(y * y, axis=-1, keepdims=True)
        y = (y * lax.rsqrt(ms + EPS)) * fgain_ref[...]
    out_ref[...] = y


def _attention_output(q, k, v, za_silu, yf_lo, yf_hi, x2, layer, sinks, gains, w_in, bias, w_pf, w_pa,
                      w_o, final_gain, final, seq):
    tokens = x2.shape[0]
    tm = TOKEN_TILE
    assert tm == SEQ_DFT and yf_lo.shape == (tokens // tm, HALF_DFT, F_WIDTH)
    per = tm // BLOCK
    n_blocks = seq // BLOCK
    tiles_per_seq = seq // tm
    row = lambda i: (i, 0)
    const2 = lambda i: (0, 0)
    of_layer = lambda i: (layer, 0, 0)
    half = pl.BlockSpec((1, HALF_DFT, F_WIDTH), lambda i: (i, 0, 0))

    def prev_block(i):
        return (jnp.maximum(i * per - 1, (i // tiles_per_seq) * n_blocks), 0)

    def next_block(i):
        return (jnp.minimum((i + 1) * per, (i // tiles_per_seq + 1) * n_blocks - 1), 0)

    prev = pl.BlockSpec((BLOCK, KV_WIDTH), prev_block)
    cur = pl.BlockSpec((tm, KV_WIDTH), row)
    nxt = pl.BlockSpec((BLOCK, KV_WIDTH), next_block)
    wide = pl.BlockSpec((tm, A_WIDTH), row)
    return pl.pallas_call(
        functools.partial(_attn_out_kernel, final, layer, n_blocks),
        grid=(tokens // tm,),
        in_specs=[
            pl.BlockSpec(memory_space=pltpu.SMEM),
            wide, prev, cur, nxt, prev, cur, nxt, wide,
            half, half,
            pl.BlockSpec((tm, D_MODEL), row),
            pl.BlockSpec((1, 1, D_MODEL), of_layer),
            pl.BlockSpec((1, D_MODEL, GATE_BLOCK), lambda i: (layer, 0, 1), pipeline_mode=pl.Buffered(1)),
            pl.BlockSpec((1, 1, GATE_WIDTH), of_layer),
            pl.BlockSpec((1, F_WIDTH, D_MODEL), of_layer, pipeline_mode=pl.Buffered(1)),
            pl.BlockSpec((1, A_WIDTH, D_MODEL), of_layer, pipeline_mode=pl.Buffered(1)),
            pl.BlockSpec((1, D_MODEL, D_MODEL), of_layer, pipeline_mode=pl.Buffered(1)),
            pl.BlockSpec((1, D_MODEL), const2),
        ],
        out_specs=pl.BlockSpec((tm, D_MODEL), row),
        out_shape=jax.ShapeDtypeStruct((tokens, D_MODEL), F32),
        scratch_shapes=[pltpu.VMEM((tm, A_WIDTH), BF16), pltpu.VMEM((tm, GATE_WIDTH), BF16)],
        compiler_params=pltpu.CompilerParams(
            dimension_semantics=("arbitrary",), vmem_limit_bytes=VMEM_LIMIT),
        name="attention_output",
    )(sinks, q, k, k, k, v, v, v, za_silu, yf_lo, yf_hi, x2, gains, w_in, bias, w_pf, w_pa, w_o,
      final_gain.reshape(1, D_MODEL))


def _trunk(x, gains, w_in, bias, sinks, w_pf, w_pa, w_o, final_gain):
    batch, seq, _ = x.shape
    depth = gains.shape[0]
    x2 = x.reshape(batch * seq, D_MODEL)
    for l in range(depth):
        uf, zf, q, k, v, za = _in_projection(x2, l, gains, w_in, seq)
        yf_lo, yf_hi = _fourier_mix(uf, zf, batch, seq)
        x2 = _attention_output(q, k, v, za, yf_lo, yf_hi, x2, l, sinks, gains, w_in, bias, w_pf, w_pa,
                               w_o, final_gain, l == depth - 1, seq)
    return x2.reshape(batch, seq, D_MODEL)


def _prepare(norm_gain, w_in, gate_bias, sink_logit, w_proj_fourier, w_proj_attn, w_out):
    depth = norm_gain.shape[0]
    return (norm_gain.reshape(depth, 1, D_MODEL),
            w_in.astype(BF16),
            gate_bias.reshape(depth, 1, GATE_WIDTH),
            sink_logit.reshape(depth * N_HEADS),
            w_proj_fourier.astype(BF16), w_proj_attn.astype(BF16), w_out.astype(BF16))


def kernel(x_prompt, x_sample, norm_gain, w_in, gate_bias, sink_logit, w_proj_fourier, w_proj_attn,
           w_out, final_norm_gain):
    params = _prepare(norm_gain, w_in, gate_bias, sink_logit, w_proj_fourier, w_proj_attn, w_out)
    return (_trunk(x_prompt, *params, final_norm_gain), _trunk(x_sample, *params, final_norm_gain))
```

```python
import functools
import math

import numpy as np
import jax
import jax.numpy as jnp
from jax import lax
from jax.experimental import pallas as pl
from jax.experimental.pallas import tpu as pltpu

F32 = jnp.float32
BF16 = jnp.bfloat16

D_MODEL = 1024
N_FGROUPS = 4
FGROUP = 128
F_WIDTH = N_FGROUPS * FGROUP
HEAD_DIM = 128
N_HEADS = 8
N_KV = 2
GROUP = N_HEADS // N_KV
A_WIDTH = N_HEADS * HEAD_DIM
KV_WIDTH = N_KV * HEAD_DIM
WINDOW = 128
BLOCK = 128
ROT_DIM = HEAD_DIM // 4
ROPE_THETA = 500000.0
EPS = 1e-6
NEG = -1e30
GATE_WIDTH = 2 * D_MODEL
OFF_UF = 0
OFF_ZF = OFF_UF + F_WIDTH
OFF_Q = OFF_ZF + F_WIDTH
OFF_K = OFF_Q + A_WIDTH
OFF_V = OFF_K + KV_WIDTH
OFF_ZA = OFF_V + KV_WIDTH
OFF_G = OFF_ZA + A_WIDTH
IN_WIDTH = OFF_G + GATE_WIDTH
GATE_BLOCK = IN_WIDTH // 2
GATE_SKIP = OFF_G - GATE_BLOCK

LOG2E = math.log2(math.e)
ONES_ROWS = 16

LANES = 128
SEQ_DFT = 1024
HALF_DFT = SEQ_DFT // 2
DFT_PAD = 16
REV_BLOCK = 128
TOKEN_TILE = 1024
DFT_ROW_TILE = 512
RADIX_ROW_TILE = 128
VMEM_LIMIT = 56 * 1024 * 1024


def _silu(x):
    return x * jax.nn.sigmoid(x)


def _rope_tables(seq, tile):
    half = ROT_DIM // 2
    inv = 1.0 / (ROPE_THETA ** (np.arange(half, dtype=np.float64) * 2.0 / ROT_DIM))
    lane_inv = np.zeros((LANES,), np.float64)
    lane_inv[:half] = inv
    lane_inv[half:ROT_DIM] = inv
    rows = np.arange(tile, dtype=np.float64)[:, None] * lane_inv[None, :]
    bases = (np.arange(seq // tile, dtype=np.float64) * tile)[:, None] * lane_inv[None, :]
    bases = np.repeat(bases[:, None, :], 8, axis=1)
    return (jnp.asarray(np.cos(rows), F32), jnp.asarray(np.sin(rows), F32),
            jnp.asarray(np.cos(bases), F32), jnp.asarray(np.sin(bases), F32))


def _rmsnorm_bf16(x, gain):
    ms = jnp.mean(x * x, axis=-1, keepdims=True)
    return ((x * lax.rsqrt(ms + EPS)) * gain).astype(BF16)


def _inproj_kernel(radix, x_ref, gain_ref, w_ref, cr_ref, sr_ref, cb_ref, sb_ref,
                   uf_ref, zf_ref, q_ref, k_ref, v_ref, za_ref, u_scratch):
    h = _rmsnorm_bf16(x_ref[...], gain_ref[0])

    def proj(c0, n):
        return jnp.dot(h, w_ref[0, :, c0:c0 + n], preferred_element_type=F32)

    cb = cb_ref[0, 0:1, :]
    sb = sb_ref[0, 0:1, :]
    cr = cr_ref[...]
    sr = sr_ref[...]
    cos = cb * cr - sb * sr
    sin = sb * cr + cb * sr
    lane = lax.broadcasted_iota(jnp.int32, (1, LANES), 1)
    sin_lo = jnp.where(lane < ROT_DIM // 2, -sin, 0.0)
    sin_hi = jnp.where(lane >= ROT_DIM // 2, sin, 0.0)

    def rope(v):
        return (v * cos + pltpu.roll(v, LANES - ROT_DIM // 2, 1) * sin_lo
                + pltpu.roll(v, ROT_DIM // 2, 1) * sin_hi)

    kk = proj(OFF_K, KV_WIDTH)
    for hh in range(N_KV):
        lo = hh * HEAD_DIM
        k_ref[:, lo:lo + HEAD_DIM] = rope(kk[:, lo:lo + HEAD_DIM]).astype(BF16)
    scale = HEAD_DIM ** -0.5 * LOG2E
    for c in range(A_WIDTH // 512):
        qc = proj(OFF_Q + 512 * c, 512)
        for hh in range(512 // HEAD_DIM):
            lo = hh * HEAD_DIM
            q_ref[:, 512 * c + lo:512 * c + lo + HEAD_DIM] = (
                rope(qc[:, lo:lo + HEAD_DIM]) * scale).astype(BF16)
    for c in range(A_WIDTH // 512):
        za_ref[:, 512 * c:512 * (c + 1)] = _silu(proj(OFF_ZA + 512 * c, 512)).astype(BF16)
    zf_ref[...] = _silu(proj(OFF_ZF, F_WIDTH)).astype(BF16)
    u = proj(OFF_UF, F_WIDTH)
    rows = u.shape[0] // radix
    for g in range(N_FGROUPS):
        u_scratch[g] = u[:, g * FGROUP:(g + 1) * FGROUP]
    for n1 in range(radix):
        for g in range(N_FGROUPS):
            lo = n1 * F_WIDTH + g * FGROUP
            uf_ref[:, lo:lo + FGROUP] = (
                u_scratch[g, pl.ds(n1, rows, stride=radix), :].astype(BF16))
    v_ref[...] = proj(OFF_V, KV_WIDTH).astype(BF16)


def _in_projection(x2, layer, gains, w_in, seq):
    tokens = x2.shape[0]
    tm = TOKEN_TILE
    tiles_per_seq = seq // tm
    cr, sr, cb, sb = _rope_tables(seq, tm)
    row = lambda i: (i, 0)
    const2 = lambda i: (0, 0)
    radix = seq // SEQ_DFT
    widths = (F_WIDTH, A_WIDTH, KV_WIDTH, KV_WIDTH, A_WIDTH)
    return pl.pallas_call(
        functools.partial(_inproj_kernel, radix),
        grid=(tokens // tm,),
        in_specs=[
            pl.BlockSpec((tm, D_MODEL), row),
            pl.BlockSpec((1, 1, D_MODEL), lambda i: (layer, 0, 0)),
            pl.BlockSpec((1, D_MODEL, OFF_G), lambda i: (layer, 0, 0), pipeline_mode=pl.Buffered(1)),
            pl.BlockSpec((tm, LANES), const2),
            pl.BlockSpec((tm, LANES), const2),
            pl.BlockSpec((1, 8, LANES), lambda i: (i % tiles_per_seq, 0, 0)),
            pl.BlockSpec((1, 8, LANES), lambda i: (i % tiles_per_seq, 0, 0)),
        ],
        out_specs=([pl.BlockSpec((tm // radix, radix * F_WIDTH), row)]
                   + [pl.BlockSpec((tm, w), row) for w in widths]),
        out_shape=([jax.ShapeDtypeStruct((tokens // radix, radix * F_WIDTH), BF16)]
                   + [jax.ShapeDtypeStruct((tokens, w), BF16) for w in widths]),
        scratch_shapes=[pltpu.VMEM((N_FGROUPS, tm, FGROUP), F32)],
        compiler_params=pltpu.CompilerParams(
            dimension_semantics=("arbitrary",), vmem_limit_bytes=VMEM_LIMIT),
        name="in_projection",
    )(x2, gains, w_in, cr, sr, cb, sb)


def _dft_tables(seq, tr):
    n = SEQ_DFT
    radix = seq // n
    tiles = HALF_DFT // tr
    rows = tr + DFT_PAD
    k = np.arange(HALF_DFT + DFT_PAD, dtype=np.int64)
    m = np.arange(n, dtype=np.int64)
    ang = 2.0 * np.pi * ((k[:, None] * m[None, :]) % n).astype(np.float64) / n
    cs = np.stack([np.cos(ang), np.sin(ang)])
    cidx = np.arange(FGROUP, dtype=np.int64)
    cang = 2.0 * np.pi * ((cidx[:, None] * cidx[None, :]) % FGROUP).astype(np.float64) / FGROUP
    cc, sc = np.cos(cang), np.sin(cang)
    norm = 1.0 / math.sqrt(float(seq) * FGROUP)
    chan = np.block([[cc, cc], [sc, -sc]]) * norm
    rev_n = min(REV_BLOCK, tr)
    rev = np.zeros((rev_n, rev_n), np.float64)
    rev[np.arange(1, rev_n), rev_n - np.arange(1, rev_n)] = 1.0
    n1 = np.arange(radix, dtype=np.int64)
    i = np.arange(rows, dtype=np.int64)
    tang = 2.0 * np.pi * (n1[:, None] * i[None, :]).astype(np.float64) / seq
    tw = np.repeat(np.stack([np.cos(tang), np.sin(tang)])[..., None], LANES, axis=-1)
    t0 = np.arange(tiles, dtype=np.int64) * tr
    bang = 2.0 * np.pi * ((t0[:, None] * n1[None, :]) % seq).astype(np.float64) / seq
    base = np.stack([np.cos(bang), np.sin(bang)]).reshape(-1)
    return (jnp.asarray(cs, F32).astype(BF16), jnp.asarray(chan, F32).astype(BF16),
            jnp.asarray(rev, F32).astype(BF16), jnp.asarray(tw, F32), jnp.asarray(base, F32))


def _radix2_fft(re, im):
    n = len(re)
    if n == 1:
        return re, im
    er, ei = _radix2_fft(re[0::2], im[0::2])
    orr, oi = _radix2_fft(re[1::2], im[1::2])
    out_r = [None] * n
    out_i = [None] * n
    for k in range(n // 2):
        ang = -2.0 * math.pi * k / n
        c, s = math.cos(ang), math.sin(ang)
        if k == 0:
            tr, ti = orr[k], oi[k]
        elif 4 * k == n:
            tr, ti = oi[k], -orr[k]
        else:
            tr = orr[k] * c - oi[k] * s
            ti = orr[k] * s + oi[k] * c
        out_r[k] = er[k] + tr
        out_i[k] = ei[k] + ti
        out_r[k + n // 2] = er[k] - tr
        out_i[k + n // 2] = ei[k] - ti
    return out_r, out_i


def _fourier_kernel(radix, tr, tiles, base_ref, cs_ref, chan_ref, rev_ref, tw_ref, u_ref,
                    zlo_ref, zhi_ref, lo_ref, hi_ref):
    t = pl.program_id(1)
    rows = tr + DFT_PAD
    r0 = pl.multiple_of(t * tr, tr)
    c = cs_ref[0, pl.ds(r0, rows), :]
    s = cs_ref[1, pl.ds(r0, rows), :]
    chan = chan_ref[...]
    rev = rev_ref[...]
    rev_n = rev.shape[0]
    pass_groups = N_FGROUPS if radix <= 2 else 2
    width = pass_groups * FGROUP
    first_row = lax.broadcasted_iota(jnp.int32, (8, width), 0) == 0
    for g0 in range(0, N_FGROUPS, pass_groups):
        c0 = g0 * FGROUP
        re, im = [], []
        for n1 in range(radix):
            u = u_ref[0, :, n1 * F_WIDTH + c0:n1 * F_WIDTH + c0 + width]
            p = jnp.dot(c, u, preferred_element_type=F32)
            q = jnp.dot(s, u, preferred_element_type=F32)
            if n1 == 0:
                re.append(p)
                im.append(-q)
                continue
            ct, st = tw_ref[0, n1], tw_ref[1, n1]
            if tiles > 1:
                cb = base_ref[t * radix + n1]
                sb = base_ref[(tiles + t) * radix + n1]
                ct, st = cb * ct - sb * st, sb * ct + cb * st
            ct = jnp.concatenate([ct] * pass_groups, axis=1)
            st = jnp.concatenate([st] * pass_groups, axis=1)
            re.append(p * ct - q * st)
            im.append(-(p * st + q * ct))
        xr, xi = _radix2_fft(re, im)
        for k1 in range(radix):
            xrb = xr[k1].astype(BF16)
            xib = xi[k1].astype(BF16)
            y_lo, y_hi = [], []
            for g in range(pass_groups):
                lo = g * FGROUP
                lhs = jnp.concatenate([xrb[:, lo:lo + FGROUP], xib[:, lo:lo + FGROUP]], axis=1)
                z = jnp.dot(lhs, chan, preferred_element_type=F32)
                y_lo.append(z[:tr, :FGROUP])
                y_hi.append(z[:, FGROUP:])
            y_lo = jnp.concatenate(y_lo, axis=1)
            lo_ref[0, k1, :, c0:c0 + width] = (
                y_lo * zlo_ref[0, k1, 0, :, c0:c0 + width].astype(F32)).astype(BF16)
            y_hi = jnp.concatenate(y_hi, axis=1).astype(BF16)
            km = radix - 1 - k1
            for b in range(tr // rev_n):
                top = tr - rev_n * b
                blk = jnp.dot(rev, y_hi[top - rev_n:top], preferred_element_type=F32)
                head = jnp.where(first_row, y_hi[top:top + 16].astype(F32)[:8], blk[:8])
                blk = jnp.concatenate([head, blk[8:]], axis=0)
                z_hi = zhi_ref[0, km, 0, rev_n * b:rev_n * (b + 1), c0:c0 + width].astype(F32)
                hi_ref[0, km, rev_n * b:rev_n * (b + 1), c0:c0 + width] = (blk * z_hi).astype(BF16)


def _fourier_mix(u_il, zf_silu, batch, seq):
    n = SEQ_DFT
    radix = seq // n
    tr = DFT_ROW_TILE if radix <= 2 else RADIX_ROW_TILE
    tiles = HALF_DFT // tr
    cs, chan, rev, tw, base = _dft_tables(seq, tr)
    u3 = u_il.reshape(batch, n, radix * F_WIDTH)
    z5 = zf_silu.reshape(batch, radix, 2, HALF_DFT, F_WIDTH)
    u_mode = pl.Buffered(1) if batch == 1 else pl.Buffered(2)
    const = lambda nd: (lambda b, t: (0,) * nd)
    half_shape = jax.ShapeDtypeStruct((batch, radix, HALF_DFT, F_WIDTH), BF16)
    lo, hi = pl.pallas_call(
        functools.partial(_fourier_kernel, radix, tr, tiles),
        grid=(batch, tiles),
        in_specs=[
            pl.BlockSpec(memory_space=pltpu.SMEM),
            pl.BlockSpec(cs.shape, const(3), pipeline_mode=pl.Buffered(1)),
            pl.BlockSpec(chan.shape, const(2)),
            pl.BlockSpec(rev.shape, const(2)),
            pl.BlockSpec(tw.shape, const(4), pipeline_mode=pl.Buffered(1)),
            pl.BlockSpec((1, n, radix * F_WIDTH), lambda b, t: (b, 0, 0), pipeline_mode=u_mode),
            pl.BlockSpec((1, radix, 1, tr, F_WIDTH), lambda b, t: (b, 0, 0, t, 0)),
            pl.BlockSpec((1, radix, 1, tr, F_WIDTH), lambda b, t: (b, 0, 1, tiles - 1 - t, 0)),
        ],
        out_specs=[
            pl.BlockSpec((1, radix, tr, F_WIDTH), lambda b, t: (b, 0, t, 0)),
            pl.BlockSpec((1, radix, tr, F_WIDTH), lambda b, t: (b, 0, tiles - 1 - t, 0)),
        ],
        out_shape=[half_shape, half_shape],
        compiler_params=pltpu.CompilerParams(
            dimension_semantics=("arbitrary", "arbitrary"), vmem_limit_bytes=VMEM_LIMIT),
        name="fourier_mix",
    )(base, cs, chan, rev, tw, u3, z5, z5)
    return (lo.reshape(batch * radix, HALF_DFT, F_WIDTH), hi.reshape(batch * radix, HALF_DFT, F_WIDTH))


def _attn_out_kernel(final, layer, n_blocks, sink_ref, q_ref, kp_ref, kc_ref, kn_ref, vp_ref, vc_ref,
                     vn_ref, za_ref, ylo_ref, yhi_ref, x_ref, gain_ref, wg_ref, bias_ref, wpf_ref,
                     wpa_ref, wo_ref, fgain_ref, out_ref, ya_scratch, g_scratch):
    tm = TOKEN_TILE
    blocks = tm // BLOCK
    tiles_per_seq = n_blocks // blocks
    ti = pl.program_id(0) % tiles_per_seq
    x = x_ref[...]
    h = _rmsnorm_bf16(x, gain_ref[0])
    kcat = jnp.concatenate([kp_ref[...], kc_ref[...], kn_ref[...]], axis=0)
    vcat = jnp.concatenate([vp_ref[...], vc_ref[...], vn_ref[...]], axis=0)
    vcat_t = vcat.astype(F32).T.astype(BF16)
    ones_rows = jnp.ones((ONES_ROWS, 3 * BLOCK), BF16)
    d = (lax.broadcasted_iota(jnp.int32, (BLOCK, BLOCK), 0)
         - lax.broadcasted_iota(jnp.int32, (BLOCK, BLOCK), 1))
    d = jnp.concatenate([d] * N_HEADS, axis=1)
    gate_chunk = GATE_WIDTH // blocks

    def scores(j):
        cols = []
        for g in range(N_KV):
            kk = kcat[j * BLOCK:(j + 3) * BLOCK, g * HEAD_DIM:(g + 1) * HEAD_DIM]
            q4 = jnp.concatenate(
                [q_ref[j * BLOCK:(j + 1) * BLOCK, hd * HEAD_DIM:(hd + 1) * HEAD_DIM]
                 for hd in range(GROUP * g, GROUP * (g + 1))], axis=0)
            cols.append(lax.dot_general(kk, q4, (((1,), (1,)), ((), ())), preferred_element_type=F32))
        return jnp.concatenate(cols, axis=1)

    s_next = scores(0)
    for j in range(blocks):
        s = s_next
        if j + 1 < blocks:
            s_next = scores(j + 1)
        c0 = j * gate_chunk
        pre = jnp.dot(h, wg_ref[0, :, GATE_SKIP + c0:GATE_SKIP + c0 + gate_chunk],
                      preferred_element_type=F32)
        g_scratch[:, c0:c0 + gate_chunk] = jax.nn.sigmoid(
            pre + bias_ref[0, :, c0:c0 + gate_chunk]).astype(BF16)
        bj = ti * blocks + j
        lo = jnp.where(bj > 0, 0, BLOCK)
        hi = jnp.where(bj < n_blocks - 1, 0, -BLOCK)
        top = jnp.where(d >= lo, s[:BLOCK], NEG)
        mid = s[BLOCK:2 * BLOCK]
        bot = jnp.where(d <= hi, s[2 * BLOCK:], NEG)
        sk = jnp.concatenate(
            [jnp.full((1, BLOCK), sink_ref[layer * N_HEADS + hd] * LOG2E, F32) for hd in range(N_HEADS)],
            axis=1)
        m = jnp.maximum(jnp.maximum(jnp.max(top, axis=0, keepdims=True),
                                    jnp.max(mid, axis=0, keepdims=True)),
                        jnp.maximum(jnp.max(bot, axis=0, keepdims=True), sk))
        p = jnp.concatenate([jnp.exp2(top - m), jnp.exp2(mid - m), jnp.exp2(bot - m)],
                            axis=0).astype(BF16)
        p_sink = jnp.exp2(sk - m)
        for g in range(N_KV):
            v_t = vcat_t[g * HEAD_DIM:(g + 1) * HEAD_DIM, j * BLOCK:(j + 3) * BLOCK]
            v_aug = jnp.concatenate([v_t, ones_rows], axis=0)
            w = GROUP * BLOCK
            o_aug = jnp.dot(v_aug, p[:, g * w:(g + 1) * w], preferred_element_type=F32)
            denom = o_aug[HEAD_DIM:HEAD_DIM + 1] + p_sink[:, g * w:(g + 1) * w]
            o_t = o_aug[:HEAD_DIM] * (1.0 / denom)
            for hh in range(GROUP):
                hd = GROUP * g + hh
                o = o_t[:, hh * BLOCK:(hh + 1) * BLOCK].T
                za = za_ref[j * BLOCK:(j + 1) * BLOCK, hd * HEAD_DIM:(hd + 1) * HEAD_DIM]
                ya_scratch[j * BLOCK:(j + 1) * BLOCK, hd * HEAD_DIM:(hd + 1) * HEAD_DIM] = (
                    o * za.astype(F32)).astype(BF16)

    y_f = jnp.concatenate([ylo_ref[0], yhi_ref[0]], axis=0)
    br_f = jnp.dot(y_f, wpf_ref[0], preferred_element_type=F32)
    br_a = jnp.dot(ya_scratch[...], wpa_ref[0], preferred_element_type=F32)
    merged = (g_scratch[:, :D_MODEL].astype(F32) * br_f
              + g_scratch[:, D_MODEL:].astype(F32) * br_a).astype(BF16)
    y = x + jnp.dot(merged, wo_ref[0], preferred_element_type=F32)
    if final:
        ms = jnp.mean(y * y, axis=-1, keepdims=True)
        y = (y * lax.rsqrt(ms + EPS)) * fgain_ref[...]
    out_ref[...] = y


def _attention_output(q, k, v, za_silu, yf_lo, yf_hi, x2, layer, sinks, gains, w_in, bias, w_pf, w_pa,
                      w_o, final_gain, final, seq):
    tokens = x2.shape[0]
    tm = TOKEN_TILE
    assert tm == SEQ_DFT and yf_lo.shape == (tokens // tm, HALF_DFT, F_WIDTH)
    per = tm // BLOCK
    n_blocks = seq // BLOCK
    tiles_per_seq = seq // tm
    row = lambda i: (i, 0)
    const2 = lambda i: (0, 0)
    of_layer = lambda i: (layer, 0, 0)
    half = pl.BlockSpec((1, HALF_DFT, F_WIDTH), lambda i: (i, 0, 0))

    def prev_block(i):
        return (jnp.maximum(i * per - 1, (i // tiles_per_seq) * n_blocks), 0)

    def next_block(i):
        return (jnp.minimum((i + 1) * per, (i // tiles_per_seq + 1) * n_blocks - 1), 0)

    prev = pl.BlockSpec((BLOCK, KV_WIDTH), prev_block)
    cur = pl.BlockSpec((tm, KV_WIDTH), row)
    nxt = pl.BlockSpec((BLOCK, KV_WIDTH), next_block)
    wide = pl.BlockSpec((tm, A_WIDTH), row)
    return pl.pallas_call(
        functools.partial(_attn_out_kernel, final, layer, n_blocks),
        grid=(tokens // tm,),
        in_specs=[
            pl.BlockSpec(memory_space=pltpu.SMEM),
            wide, prev, cur, nxt, prev, cur, nxt, wide,
            half, half,
            pl.BlockSpec((tm, D_MODEL), row),
            pl.BlockSpec((1, 1, D_MODEL), of_layer),
            pl.BlockSpec((1, D_MODEL, GATE_BLOCK), lambda i: (layer, 0, 1), pipeline_mode=pl.Buffered(1)),
            pl.BlockSpec((1, 1, GATE_WIDTH), of_layer),
            pl.BlockSpec((1, F_WIDTH, D_MODEL), of_layer, pipeline_mode=pl.Buffered(1)),
            pl.BlockSpec((1, A_WIDTH, D_MODEL), of_layer, pipeline_mode=pl.Buffered(1)),
            pl.BlockSpec((1, D_MODEL, D_MODEL), of_layer, pipeline_mode=pl.Buffered(1)),
            pl.BlockSpec((1, D_MODEL), const2),
        ],
        out_specs=pl.BlockSpec((tm, D_MODEL), row),
        out_shape=jax.ShapeDtypeStruct((tokens, D_MODEL), F32),
        scratch_shapes=[pltpu.VMEM((tm, A_WIDTH), BF16), pltpu.VMEM((tm, GATE_WIDTH), BF16)],
        compiler_params=pltpu.CompilerParams(
            dimension_semantics=("arbitrary",), vmem_limit_bytes=VMEM_LIMIT),
        name="attention_output",
    )(sinks, q, k, k, k, v, v, v, za_silu, yf_lo, yf_hi, x2, gains, w_in, bias, w_pf, w_pa, w_o,
      final_gain.reshape(1, D_MODEL))


def _trunk(x, gains, w_in, bias, sinks, w_pf, w_pa, w_o, final_gain):
    batch, seq, _ = x.shape
    depth = gains.shape[0]
    x2 = x.reshape(batch * seq, D_MODEL)
    for l in range(depth):
        uf, zf, q, k, v, za = _in_projection(x2, l, gains, w_in, seq)
        yf_lo, yf_hi = _fourier_mix(uf, zf, batch, seq)
        x2 = _attention_output(q, k, v, za, yf_lo, yf_hi, x2, l, sinks, gains, w_in, bias, w_pf, w_pa,
                               w_o, final_gain, l == depth - 1, seq)
    return x2.reshape(batch, seq, D_MODEL)


def _prepare(norm_gain, w_in, gate_bias, sink_logit, w_proj_fourier, w_proj_attn, w_out):
    depth = norm_gain.shape[0]
    return (norm_gain.reshape(depth, 1, D_MODEL),
            w_in.astype(BF16),
            gate_bias.reshape(depth, 1, GATE_WIDTH),
            sink_logit.reshape(depth * N_HEADS),
            w_proj_fourier.astype(BF16), w_proj_attn.astype(BF16), w_out.astype(BF16))


def kernel(x_prompt, x_sample, norm_gain, w_in, gate_bias, sink_logit, w_proj_fourier, w_proj_attn,
           w_out, final_norm_gain):
    params = _prepare(norm_gain, w_in, gate_bias, sink_logit, w_proj_fourier, w_proj_attn, w_out)
    return (_trunk(x_prompt, *params, final_norm_gain), _trunk(x_sample, *params, final_norm_gain))
```

```python
import functools
import math

import numpy as np
import jax
import jax.numpy as jnp
from jax import lax
from jax.experimental import pallas as pl
from jax.experimental.pallas import tpu as pltpu

F32 = jnp.float32
BF16 = jnp.bfloat16

D_MODEL = 1024
N_FGROUPS = 4
FGROUP = 128
F_WIDTH = N_FGROUPS * FGROUP
HEAD_DIM = 128
N_HEADS = 8
N_KV = 2
GROUP = N_HEADS // N_KV
A_WIDTH = N_HEADS * HEAD_DIM
KV_WIDTH = N_KV * HEAD_DIM
WINDOW = 128
BLOCK = 128
ROT_DIM = HEAD_DIM // 4
ROPE_THETA = 500000.0
EPS = 1e-6
NEG = -1e30
GATE_WIDTH = 2 * D_MODEL
OFF_UF = 0
OFF_ZF = OFF_UF + F_WIDTH
OFF_Q = OFF_ZF + F_WIDTH
OFF_K = OFF_Q + A_WIDTH
OFF_V = OFF_K + KV_WIDTH
OFF_ZA = OFF_V + KV_WIDTH
OFF_G = OFF_ZA + A_WIDTH
IN_WIDTH = OFF_G + GATE_WIDTH
GATE_BLOCK = IN_WIDTH // 2
GATE_SKIP = OFF_G - GATE_BLOCK

LOG2E = math.log2(math.e)
ONES_ROWS = 16

LANES = 128
MAX_SEQ_DFT = 1024
MIN_SEQ_DFT = 512
MAX_SHORT_RADIX = 4
DFT_PAD = 16
REV_BLOCK = 128
TOKEN_TILE = 1024
DFT_ROW_TILE = 512
FOURIER_BATCH = 2
RADIX_ROW_TILE = 128
VMEM_LIMIT = 56 * 1024 * 1024


def _seq_dft(seq):
    return MIN_SEQ_DFT if seq // MIN_SEQ_DFT <= MAX_SHORT_RADIX else MAX_SEQ_DFT


def _silu(x):
    return x * jax.nn.sigmoid(x)


def _rope_tables(seq, tile):
    half = ROT_DIM // 2
    inv = 1.0 / (ROPE_THETA ** (np.arange(half, dtype=np.float64) * 2.0 / ROT_DIM))
    lane_inv = np.zeros((LANES,), np.float64)
    lane_inv[:half] = inv
    lane_inv[half:ROT_DIM] = inv
    rows = np.arange(tile, dtype=np.float64)[:, None] * lane_inv[None, :]
    bases = (np.arange(seq // tile, dtype=np.float64) * tile)[:, None] * lane_inv[None, :]
    bases = np.repeat(bases[:, None, :], 8, axis=1)
    return (jnp.asarray(np.cos(rows), F32), jnp.asarray(np.sin(rows), F32),
            jnp.asarray(np.cos(bases), F32), jnp.asarray(np.sin(bases), F32))


def _rmsnorm_bf16(x, gain):
    ms = jnp.mean(x * x, axis=-1, keepdims=True)
    return ((x * lax.rsqrt(ms + EPS)) * gain).astype(BF16)


def _inproj_kernel(radix, x_ref, gain_ref, w_ref, cr_ref, sr_ref, cb_ref, sb_ref,
                   uf_ref, zf_ref, q_ref, k_ref, v_ref, za_ref, u_scratch):
    h = _rmsnorm_bf16(x_ref[...], gain_ref[0])

    def proj(c0, n):
        return jnp.dot(h, w_ref[0, :, c0:c0 + n], preferred_element_type=F32)

    cb = cb_ref[0, 0:1, :]
    sb = sb_ref[0, 0:1, :]
    cr = cr_ref[...]
    sr = sr_ref[...]
    cos = cb * cr - sb * sr
    sin = sb * cr + cb * sr
    lane = lax.broadcasted_iota(jnp.int32, (1, LANES), 1)
    sin_lo = jnp.where(lane < ROT_DIM // 2, -sin, 0.0)
    sin_hi = jnp.where(lane >= ROT_DIM // 2, sin, 0.0)

    def rope(v):
        return (v * cos + pltpu.roll(v, LANES - ROT_DIM // 2, 1) * sin_lo
                + pltpu.roll(v, ROT_DIM // 2, 1) * sin_hi)

    kk = proj(OFF_K, KV_WIDTH)
    for hh in range(N_KV):
        lo = hh * HEAD_DIM
        k_ref[:, lo:lo + HEAD_DIM] = rope(kk[:, lo:lo + HEAD_DIM]).astype(BF16)
    scale = HEAD_DIM ** -0.5 * LOG2E
    for c in range(A_WIDTH // 512):
        qc = proj(OFF_Q + 512 * c, 512)
        for hh in range(512 // HEAD_DIM):
            lo = hh * HEAD_DIM
            q_ref[:, 512 * c + lo:512 * c + lo + HEAD_DIM] = (
                rope(qc[:, lo:lo + HEAD_DIM]) * scale).astype(BF16)
    for c in range(A_WIDTH // 512):
        za_ref[:, 512 * c:512 * (c + 1)] = _silu(proj(OFF_ZA + 512 * c, 512)).astype(BF16)
    zf_ref[...] = _silu(proj(OFF_ZF, F_WIDTH)).astype(BF16)
    u = proj(OFF_UF, F_WIDTH)
    rows = u.shape[0] // radix
    for g in range(N_FGROUPS):
        u_scratch[g] = u[:, g * FGROUP:(g + 1) * FGROUP]
    for n1 in range(radix):
        for g in range(N_FGROUPS):
            lo = n1 * F_WIDTH + g * FGROUP
            uf_ref[:, lo:lo + FGROUP] = (
                u_scratch[g, pl.ds(n1, rows, stride=radix), :].astype(BF16))
    v_ref[...] = proj(OFF_V, KV_WIDTH).astype(BF16)


def _in_projection(x2, layer, gains, w_in, seq):
    tokens = x2.shape[0]
    tm = TOKEN_TILE
    tiles_per_seq = seq // tm
    cr, sr, cb, sb = _rope_tables(seq, tm)
    row = lambda i: (i, 0)
    const2 = lambda i: (0, 0)
    radix = seq // _seq_dft(seq)
    widths = (F_WIDTH, A_WIDTH, KV_WIDTH, KV_WIDTH, A_WIDTH)
    return pl.pallas_call(
        functools.partial(_inproj_kernel, radix),
        grid=(tokens // tm,),
        in_specs=[
            pl.BlockSpec((tm, D_MODEL), row),
            pl.BlockSpec((1, 1, D_MODEL), lambda i: (layer, 0, 0)),
            pl.BlockSpec((1, D_MODEL, OFF_G), lambda i: (layer, 0, 0), pipeline_mode=pl.Buffered(1)),
            pl.BlockSpec((tm, LANES), const2),
            pl.BlockSpec((tm, LANES), const2),
            pl.BlockSpec((1, 8, LANES), lambda i: (i % tiles_per_seq, 0, 0)),
            pl.BlockSpec((1, 8, LANES), lambda i: (i % tiles_per_seq, 0, 0)),
        ],
        out_specs=([pl.BlockSpec((tm // radix, radix * F_WIDTH), row)]
                   + [pl.BlockSpec((tm, w), row) for w in widths]),
        out_shape=([jax.ShapeDtypeStruct((tokens // radix, radix * F_WIDTH), BF16)]
                   + [jax.ShapeDtypeStruct((tokens, w), BF16) for w in widths]),
        scratch_shapes=[pltpu.VMEM((N_FGROUPS, tm, FGROUP), F32)],
        compiler_params=pltpu.CompilerParams(
            dimension_semantics=("arbitrary",), vmem_limit_bytes=VMEM_LIMIT),
        name="in_projection",
    )(x2, gains, w_in, cr, sr, cb, sb)


def _dft_tables(seq, tr):
    n = _seq_dft(seq)
    radix = seq // n
    tiles = n // 2 // tr
    rows = tr + DFT_PAD
    k = np.arange(n // 2 + DFT_PAD, dtype=np.int64)
    m = np.arange(n, dtype=np.int64)
    ang = 2.0 * np.pi * ((k[:, None] * m[None, :]) % n).astype(np.float64) / n
    cs = np.stack([np.cos(ang), np.sin(ang)])
    cidx = np.arange(FGROUP, dtype=np.int64)
    cang = 2.0 * np.pi * ((cidx[:, None] * cidx[None, :]) % FGROUP).astype(np.float64) / FGROUP
    cc, sc = np.cos(cang), np.sin(cang)
    norm = 1.0 / math.sqrt(float(seq) * FGROUP)
    chan = np.block([[cc, cc], [sc, -sc]]) * norm
    rev_n = min(REV_BLOCK, tr)
    rev = np.zeros((rev_n, rev_n), np.float64)
    rev[np.arange(1, rev_n), rev_n - np.arange(1, rev_n)] = 1.0
    n1 = np.arange(radix, dtype=np.int64)
    i = np.arange(rows, dtype=np.int64)
    tang = 2.0 * np.pi * (n1[:, None] * i[None, :]).astype(np.float64) / seq
    tw = np.repeat(np.stack([np.cos(tang), np.sin(tang)])[..., None], LANES, axis=-1)
    t0 = np.arange(tiles, dtype=np.int64) * tr
    bang = 2.0 * np.pi * ((t0[:, None] * n1[None, :]) % seq).astype(np.float64) / seq
    base = np.stack([np.cos(bang), np.sin(bang)]).reshape(-1)
    return (jnp.asarray(cs, F32).astype(BF16), jnp.asarray(chan, F32).astype(BF16),
            jnp.asarray(rev, F32).astype(BF16), jnp.asarray(tw, F32), jnp.asarray(base, F32))


def _radix2_fft(re, im):
    n = len(re)
    if n == 1:
        return re, im
    er, ei = _radix2_fft(re[0::2], im[0::2])
    orr, oi = _radix2_fft(re[1::2], im[1::2])
    out_r = [None] * n
    out_i = [None] * n
    for k in range(n // 2):
        ang = -2.0 * math.pi * k / n
        c, s = math.cos(ang), math.sin(ang)
        if k == 0:
            tr, ti = orr[k], oi[k]
        elif 4 * k == n:
            tr, ti = oi[k], -orr[k]
        else:
            tr = orr[k] * c - oi[k] * s
            ti = orr[k] * s + oi[k] * c
        out_r[k] = er[k] + tr
        out_i[k] = ei[k] + ti
        out_r[k + n // 2] = er[k] - tr
        out_i[k + n // 2] = ei[k] - ti
    return out_r, out_i


def _fourier_kernel(radix, tr, tiles, nb, base_ref, cs_ref, chan_ref, rev_ref, tw_ref, u_ref,
                    zlo_ref, zhi_ref, lo_ref, hi_ref):
    t = pl.program_id(1)
    rows = tr + DFT_PAD
    r0 = pl.multiple_of(t * tr, tr)
    c = cs_ref[0, pl.ds(r0, rows), :]
    s = cs_ref[1, pl.ds(r0, rows), :]
    chan = chan_ref[...]
    rev = rev_ref[...]
    rev_n = rev.shape[0]
    pass_groups = N_FGROUPS if radix <= MAX_SHORT_RADIX else 2
    width = pass_groups * FGROUP
    first_row = lax.broadcasted_iota(jnp.int32, (8, width), 0) == 0
    for bb in range(nb):
        for g0 in range(0, N_FGROUPS, pass_groups):
            c0 = g0 * FGROUP
            re, im = [], []
            for n1 in range(radix):
                u = u_ref[bb, :, n1 * F_WIDTH + c0:n1 * F_WIDTH + c0 + width]
                p = jnp.dot(c, u, preferred_element_type=F32)
                q = jnp.dot(s, u, preferred_element_type=F32)
                if n1 == 0:
                    re.append(p)
                    im.append(-q)
                    continue
                ct, st = tw_ref[0, n1], tw_ref[1, n1]
                if tiles > 1:
                    cb = base_ref[t * radix + n1]
                    sb = base_ref[(tiles + t) * radix + n1]
                    ct, st = cb * ct - sb * st, sb * ct + cb * st
                ct = jnp.concatenate([ct] * pass_groups, axis=1)
                st = jnp.concatenate([st] * pass_groups, axis=1)
                re.append(p * ct - q * st)
                im.append(-(p * st + q * ct))
            xr, xi = _radix2_fft(re, im)
            for k1 in range(radix):
                xrb = xr[k1].astype(BF16)
                xib = xi[k1].astype(BF16)
                y_lo, y_hi = [], []
                for g in range(pass_groups):
                    lo = g * FGROUP
                    lhs = jnp.concatenate([xrb[:, lo:lo + FGROUP], xib[:, lo:lo + FGROUP]], axis=1)
                    z = jnp.dot(lhs, chan, preferred_element_type=F32)
                    y_lo.append(z[:tr, :FGROUP])
                    y_hi.append(z[:, FGROUP:])
                y_lo = jnp.concatenate(y_lo, axis=1)
                lo_ref[bb, k1, :, c0:c0 + width] = (
                    y_lo * zlo_ref[bb, k1, 0, :, c0:c0 + width].astype(F32)).astype(BF16)
                y_hi = jnp.concatenate(y_hi, axis=1).astype(BF16)
                km = radix - 1 - k1
                for b in range(tr // rev_n):
                    top = tr - rev_n * b
                    blk = jnp.dot(rev, y_hi[top - rev_n:top], preferred_element_type=F32)
                    head = jnp.where(first_row, y_hi[top:top + 16].astype(F32)[:8], blk[:8])
                    blk = jnp.concatenate([head, blk[8:]], axis=0)
                    z_hi = zhi_ref[bb, km, 0, rev_n * b:rev_n * (b + 1), c0:c0 + width].astype(F32)
                    hi_ref[bb, km, rev_n * b:rev_n * (b + 1), c0:c0 + width] = (blk * z_hi).astype(BF16)


def _fourier_mix(u_il, zf_silu, batch, seq):
    n = _seq_dft(seq)
    half = n // 2
    radix = seq // n
    tr = min(DFT_ROW_TILE, half) if radix <= MAX_SHORT_RADIX else RADIX_ROW_TILE
    tiles = half // tr
    cs, chan, rev, tw, base = _dft_tables(seq, tr)
    u3 = u_il.reshape(batch, n, radix * F_WIDTH)
    z5 = zf_silu.reshape(batch, radix, 2, half, F_WIDTH)
    u_mode = pl.Buffered(1) if batch == 1 else pl.Buffered(2)
    nb = FOURIER_BATCH if batch % FOURIER_BATCH == 0 else 1
    const = lambda nd: (lambda b, t: (0,) * nd)
    half_shape = jax.ShapeDtypeStruct((batch, radix, half, F_WIDTH), BF16)
    lo, hi = pl.pallas_call(
        functools.partial(_fourier_kernel, radix, tr, tiles, nb),
        grid=(batch // nb, tiles),
        in_specs=[
            pl.BlockSpec(memory_space=pltpu.SMEM),
            pl.BlockSpec(cs.shape, const(3), pipeline_mode=pl.Buffered(1)),
            pl.BlockSpec(chan.shape, const(2)),
            pl.BlockSpec(rev.shape, const(2)),
            pl.BlockSpec(tw.shape, const(4), pipeline_mode=pl.Buffered(1)),
            pl.BlockSpec((nb, n, radix * F_WIDTH), lambda b, t: (b, 0, 0), pipeline_mode=u_mode),
            pl.BlockSpec((nb, radix, 1, tr, F_WIDTH), lambda b, t: (b, 0, 0, t, 0)),
            pl.BlockSpec((nb, radix, 1, tr, F_WIDTH), lambda b, t: (b, 0, 1, tiles - 1 - t, 0)),
        ],
        out_specs=[
            pl.BlockSpec((nb, radix, tr, F_WIDTH), lambda b, t: (b, 0, t, 0)),
            pl.BlockSpec((nb, radix, tr, F_WIDTH), lambda b, t: (b, 0, tiles - 1 - t, 0)),
        ],
        out_shape=[half_shape, half_shape],
        compiler_params=pltpu.CompilerParams(
            dimension_semantics=("arbitrary", "arbitrary"), vmem_limit_bytes=VMEM_LIMIT),
        name="fourier_mix",
    )(base, cs, chan, rev, tw, u3, z5, z5)
    return lo.reshape(batch * radix, half, F_WIDTH), hi.reshape(batch * radix, half, F_WIDTH)


def _attn_out_kernel(final, layer, n_blocks, sink_ref, q_ref, kp_ref, kc_ref, kn_ref, vp_ref, vc_ref,
                     vn_ref, za_ref, ylo_ref, yhi_ref, x_ref, gain_ref, wg_ref, bias_ref, wpf_ref,
                     wpa_ref, wo_ref, fgain_ref, out_ref, ya_scratch, g_scratch):
    tm = TOKEN_TILE
    blocks = tm // BLOCK
    tiles_per_seq = n_blocks // blocks
    ti = pl.program_id(0) % tiles_per_seq
    x = x_ref[...]
    h = _rmsnorm_bf16(x, gain_ref[0])
    kcat = jnp.concatenate([kp_ref[...], kc_ref[...], kn_ref[...]], axis=0)
    vcat = jnp.concatenate([vp_ref[...], vc_ref[...], vn_ref[...]], axis=0)
    vcat_t = vcat.astype(F32).T.astype(BF16)
    ones_rows = jnp.ones((ONES_ROWS, 3 * BLOCK), BF16)
    d = (lax.broadcasted_iota(jnp.int32, (BLOCK, BLOCK), 0)
         - lax.broadcasted_iota(jnp.int32, (BLOCK, BLOCK), 1))
    d = jnp.concatenate([d] * N_HEADS, axis=1)
    gate_chunk = GATE_WIDTH // blocks

    def scores(j):
        cols = []
        for g in range(N_KV):
            kk = kcat[j * BLOCK:(j + 3) * BLOCK, g * HEAD_DIM:(g + 1) * HEAD_DIM]
            q4 = jnp.concatenate(
                [q_ref[j * BLOCK:(j + 1) * BLOCK, hd * HEAD_DIM:(hd + 1) * HEAD_DIM]
                 for hd in range(GROUP * g, GROUP * (g + 1))], axis=0)
            cols.append(lax.dot_general(kk, q4, (((1,), (1,)), ((), ())), preferred_element_type=F32))
        return jnp.concatenate(cols, axis=1)

    s_next = scores(0)
    for j in range(blocks):
        s = s_next
        if j + 1 < blocks:
            s_next = scores(j + 1)
        c0 = j * gate_chunk
        pre = jnp.dot(h, wg_ref[0, :, GATE_SKIP + c0:GATE_SKIP + c0 + gate_chunk],
                      preferred_element_type=F32)
        g_scratch[:, c0:c0 + gate_chunk] = jax.nn.sigmoid(
            pre + bias_ref[0, :, c0:c0 + gate_chunk]).astype(BF16)
        bj = ti * blocks + j
        lo = jnp.where(bj > 0, 0, BLOCK)
        hi = jnp.where(bj < n_blocks - 1, 0, -BLOCK)
        top = jnp.where(d >= lo, s[:BLOCK], NEG)
        mid = s[BLOCK:2 * BLOCK]
        bot = jnp.where(d <= hi, s[2 * BLOCK:], NEG)
        sk = jnp.concatenate(
            [jnp.full((1, BLOCK), sink_ref[layer * N_HEADS + hd] * LOG2E, F32) for hd in range(N_HEADS)],
            axis=1)
        m = jnp.maximum(jnp.maximum(jnp.max(top, axis=0, keepdims=True),
                                    jnp.max(mid, axis=0, keepdims=True)),
                        jnp.maximum(jnp.max(bot, axis=0, keepdims=True), sk))
        p = jnp.concatenate([jnp.exp2(top - m), jnp.exp2(mid - m), jnp.exp2(bot - m)],
                            axis=0).astype(BF16)
        p_sink = jnp.exp2(sk - m)
        for g in range(N_KV):
            v_t = vcat_t[g * HEAD_DIM:(g + 1) * HEAD_DIM, j * BLOCK:(j + 3) * BLOCK]
            v_aug = jnp.concatenate([v_t, ones_rows], axis=0)
            w = GROUP * BLOCK
            o_aug = jnp.dot(v_aug, p[:, g * w:(g + 1) * w], preferred_element_type=F32)
            denom = o_aug[HEAD_DIM:HEAD_DIM + 1] + p_sink[:, g * w:(g + 1) * w]
            o_t = o_aug[:HEAD_DIM] * (1.0 / denom)
            for hh in range(GROUP):
                hd = GROUP * g + hh
                o = o_t[:, hh * BLOCK:(hh + 1) * BLOCK].T
                za = za_ref[j * BLOCK:(j + 1) * BLOCK, hd * HEAD_DIM:(hd + 1) * HEAD_DIM]
                ya_scratch[j * BLOCK:(j + 1) * BLOCK, hd * HEAD_DIM:(hd + 1) * HEAD_DIM] = (
                    o * za.astype(F32)).astype(BF16)

    y_f = jnp.concatenate(
        [half_ref[b] for b in range(ylo_ref.shape[0]) for half_ref in (ylo_ref, yhi_ref)], axis=0)
    br_f = jnp.dot(y_f, wpf_ref[0], preferred_element_type=F32)
    br_a = jnp.dot(ya_scratch[...], wpa_ref[0], preferred_element_type=F32)
    merged = (g_scratch[:, :D_MODEL].astype(F32) * br_f
              + g_scratch[:, D_MODEL:].astype(F32) * br_a).astype(BF16)
    y = x + jnp.dot(merged, wo_ref[0], preferred_element_type=F32)
    if final:
        ms = jnp.mean(y * y, axis=-1, keepdims=True)
        y = (y * lax.rsqrt(ms + EPS)) * fgain_ref[...]
    out_ref[...] = y


def _attention_output(q, k, v, za_silu, yf_lo, yf_hi, x2, layer, sinks, gains, w_in, bias, w_pf, w_pa,
                      w_o, final_gain, final, seq):
    tokens = x2.shape[0]
    tm = TOKEN_TILE
    n = _seq_dft(seq)
    assert tm % n == 0 and yf_lo.shape == (tokens // n, n // 2, F_WIDTH)
    per = tm // BLOCK
    n_blocks = seq // BLOCK
    tiles_per_seq = seq // tm
    row = lambda i: (i, 0)
    const2 = lambda i: (0, 0)
    of_layer = lambda i: (layer, 0, 0)
    half = pl.BlockSpec((tm // n, n // 2, F_WIDTH), lambda i: (i, 0, 0))

    def prev_block(i):
        return (jnp.maximum(i * per - 1, (i // tiles_per_seq) * n_blocks), 0)

    def next_block(i):
        return (jnp.minimum((i + 1) * per, (i // tiles_per_seq + 1) * n_blocks - 1), 0)

    prev = pl.BlockSpec((BLOCK, KV_WIDTH), prev_block)
    cur = pl.BlockSpec((tm, KV_WIDTH), row)
    nxt = pl.BlockSpec((BLOCK, KV_WIDTH), next_block)
    wide = pl.BlockSpec((tm, A_WIDTH), row)
    return pl.pallas_call(
        functools.partial(_attn_out_kernel, final, layer, n_blocks),
        grid=(tokens // tm,),
        in_specs=[
            pl.BlockSpec(memory_space=pltpu.SMEM),
            wide, prev, cur, nxt, prev, cur, nxt, wide,
            half, half,
            pl.BlockSpec((tm, D_MODEL), row),
            pl.BlockSpec((1, 1, D_MODEL), of_layer),
            pl.BlockSpec((1, D_MODEL, GATE_BLOCK), lambda i: (layer, 0, 1), pipeline_mode=pl.Buffered(1)),
            pl.BlockSpec((1, 1, GATE_WIDTH), of_layer),
            pl.BlockSpec((1, F_WIDTH, D_MODEL), of_layer, pipeline_mode=pl.Buffered(1)),
            pl.BlockSpec((1, A_WIDTH, D_MODEL), of_layer, pipeline_mode=pl.Buffered(1)),
            pl.BlockSpec((1, D_MODEL, D_MODEL), of_layer, pipeline_mode=pl.Buffered(1)),
            pl.BlockSpec((1, D_MODEL), const2),
        ],
        out_specs=pl.BlockSpec((tm, D_MODEL), row),
        out_shape=jax.ShapeDtypeStruct((tokens, D_MODEL), F32),
        scratch_shapes=[pltpu.VMEM((tm, A_WIDTH), BF16), pltpu.VMEM((tm, GATE_WIDTH), BF16)],
        compiler_params=pltpu.CompilerParams(
            dimension_semantics=("arbitrary",), vmem_limit_bytes=VMEM_LIMIT),
        name="attention_output",
    )(sinks, q, k, k, k, v, v, v, za_silu, yf_lo, yf_hi, x2, gains, w_in, bias, w_pf, w_pa, w_o,
      final_gain.reshape(1, D_MODEL))


def _trunk(x, gains, w_in, bias, sinks, w_pf, w_pa, w_o, final_gain):
    batch, seq, _ = x.shape
    depth = gains.shape[0]
    x2 = x.reshape(batch * seq, D_MODEL)
    for l in range(depth):
        uf, zf, q, k, v, za = _in_projection(x2, l, gains, w_in, seq)
        yf_lo, yf_hi = _fourier_mix(uf, zf, batch, seq)
        x2 = _attention_output(q, k, v, za, yf_lo, yf_hi, x2, l, sinks, gains, w_in, bias, w_pf, w_pa,
                               w_o, final_gain, l == depth - 1, seq)
    return x2.reshape(batch, seq, D_MODEL)


def _prepare(norm_gain, w_in, gate_bias, sink_logit, w_proj_fourier, w_proj_attn, w_out):
    depth = norm_gain.shape[0]
    return (norm_gain.reshape(depth, 1, D_MODEL),
            w_in.astype(BF16),
            gate_bias.reshape(depth, 1, GATE_WIDTH),
            sink_logit.reshape(depth * N_HEADS),
            w_proj_fourier.astype(BF16), w_proj_attn.astype(BF16), w_out.astype(BF16))


def kernel(x_prompt, x_sample, norm_gain, w_in, gate_bias, sink_logit, w_proj_fourier, w_proj_attn,
           w_out, final_norm_gain):
    params = _prepare(norm_gain, w_in, gate_bias, sink_logit, w_proj_fourier, w_proj_attn, w_out)
    return (_trunk(x_prompt, *params, final_norm_gain), _trunk(x_sample, *params, final_norm_gain))
```

```python
import functools
import math

import numpy as np
import jax
import jax.numpy as jnp
from jax import lax
from jax.experimental import pallas as pl
from jax.experimental.pallas import tpu as pltpu

F32 = jnp.float32
BF16 = jnp.bfloat16

D_MODEL = 1024
N_FGROUPS = 4
FGROUP = 128
F_WIDTH = N_FGROUPS * FGROUP
HEAD_DIM = 128
N_HEADS = 8
N_KV = 2
GROUP = N_HEADS // N_KV
A_WIDTH = N_HEADS * HEAD_DIM
KV_WIDTH = N_KV * HEAD_DIM
WINDOW = 128
BLOCK = 128
ROT_DIM = HEAD_DIM // 4
ROPE_THETA = 500000.0
EPS = 1e-6
NEG = -1e30
GATE_WIDTH = 2 * D_MODEL
OFF_UF = 0
OFF_ZF = OFF_UF + F_WIDTH
OFF_Q = OFF_ZF + F_WIDTH
OFF_K = OFF_Q + A_WIDTH
OFF_V = OFF_K + KV_WIDTH
OFF_ZA = OFF_V + KV_WIDTH
OFF_G = OFF_ZA + A_WIDTH
IN_WIDTH = OFF_G + GATE_WIDTH
GATE_BLOCK = IN_WIDTH // 2
GATE_SKIP = OFF_G - GATE_BLOCK

LOG2E = math.log2(math.e)
ONES_ROWS = 16

LANES = 128
MAX_SEQ_DFT = 1024
MIN_SEQ_DFT = 512
MAX_SHORT_RADIX = 4
DFT_PAD = 16
REV_BLOCK = 128
TOKEN_TILE = 1024
DFT_ROW_TILE = 512
FOURIER_BATCH = 2
RADIX_ROW_TILE = 128
TAIL_CHUNKS = 2
VMEM_LIMIT = 56 * 1024 * 1024


def _seq_dft(seq):
    return MIN_SEQ_DFT if seq // MIN_SEQ_DFT <= MAX_SHORT_RADIX else MAX_SEQ_DFT


def _silu(x):
    return x * jax.nn.sigmoid(x)


def _rope_tables(seq, tile):
    half = ROT_DIM // 2
    inv = 1.0 / (ROPE_THETA ** (np.arange(half, dtype=np.float64) * 2.0 / ROT_DIM))
    lane_inv = np.zeros((LANES,), np.float64)
    lane_inv[:half] = inv
    lane_inv[half:ROT_DIM] = inv
    rows = np.arange(tile, dtype=np.float64)[:, None] * lane_inv[None, :]
    bases = (np.arange(seq // tile, dtype=np.float64) * tile)[:, None] * lane_inv[None, :]
    bases = np.repeat(bases[:, None, :], 8, axis=1)
    return (jnp.asarray(np.cos(rows), F32), jnp.asarray(np.sin(rows), F32),
            jnp.asarray(np.cos(bases), F32), jnp.asarray(np.sin(bases), F32))


def _rmsnorm_bf16(x, gain):
    ms = jnp.mean(x * x, axis=-1, keepdims=True)
    return ((x * lax.rsqrt(ms + EPS)) * gain).astype(BF16)


def _inproj_kernel(radix, x_ref, gain_ref, w_ref, cr_ref, sr_ref, cb_ref, sb_ref,
                   uf_ref, zf_ref, q_ref, k_ref, v_ref, za_ref, u_scratch):
    h = _rmsnorm_bf16(x_ref[...], gain_ref[0])

    def proj(c0, n):
        return jnp.dot(h, w_ref[0, :, c0:c0 + n], preferred_element_type=F32)

    cb = cb_ref[0, 0:1, :]
    sb = sb_ref[0, 0:1, :]
    cr = cr_ref[...]
    sr = sr_ref[...]
    cos = cb * cr - sb * sr
    sin = sb * cr + cb * sr
    lane = lax.broadcasted_iota(jnp.int32, (1, LANES), 1)
    sin_lo = jnp.where(lane < ROT_DIM // 2, -sin, 0.0)
    sin_hi = jnp.where(lane >= ROT_DIM // 2, sin, 0.0)

    def rope(v):
        return (v * cos + pltpu.roll(v, LANES - ROT_DIM // 2, 1) * sin_lo
                + pltpu.roll(v, ROT_DIM // 2, 1) * sin_hi)

    kk = proj(OFF_K, KV_WIDTH)
    for hh in range(N_KV):
        lo = hh * HEAD_DIM
        k_ref[:, lo:lo + HEAD_DIM] = rope(kk[:, lo:lo + HEAD_DIM]).astype(BF16)
    scale = HEAD_DIM ** -0.5 * LOG2E
    for c in range(A_WIDTH // 512):
        qc = proj(OFF_Q + 512 * c, 512)
        for hh in range(512 // HEAD_DIM):
            lo = hh * HEAD_DIM
            q_ref[:, 512 * c + lo:512 * c + lo + HEAD_DIM] = (
                rope(qc[:, lo:lo + HEAD_DIM]) * scale).astype(BF16)
    for c in range(A_WIDTH // 512):
        za_ref[:, 512 * c:512 * (c + 1)] = _silu(proj(OFF_ZA + 512 * c, 512)).astype(BF16)
    zf_ref[...] = _silu(proj(OFF_ZF, F_WIDTH)).astype(BF16)
    u = proj(OFF_UF, F_WIDTH)
    rows = u.shape[0] // radix
    for g in range(N_FGROUPS):
        u_scratch[g] = u[:, g * FGROUP:(g + 1) * FGROUP]
    for n1 in range(radix):
        for g in range(N_FGROUPS):
            lo = n1 * F_WIDTH + g * FGROUP
            uf_ref[:, lo:lo + FGROUP] = (
                u_scratch[g, pl.ds(n1, rows, stride=radix), :].astype(BF16))
    v_ref[...] = proj(OFF_V, KV_WIDTH).astype(BF16)


def _in_projection(x2, layer, gains, w_in, seq):
    tokens = x2.shape[0]
    tm = TOKEN_TILE
    tiles_per_seq = seq // tm
    cr, sr, cb, sb = _rope_tables(seq, tm)
    row = lambda i: (i, 0)
    const2 = lambda i: (0, 0)
    radix = seq // _seq_dft(seq)
    widths = (F_WIDTH, A_WIDTH, KV_WIDTH, KV_WIDTH, A_WIDTH)
    return pl.pallas_call(
        functools.partial(_inproj_kernel, radix),
        grid=(tokens // tm,),
        in_specs=[
            pl.BlockSpec((tm, D_MODEL), row),
            pl.BlockSpec((1, 1, D_MODEL), lambda i: (layer, 0, 0)),
            pl.BlockSpec((1, D_MODEL, OFF_G), lambda i: (layer, 0, 0), pipeline_mode=pl.Buffered(1)),
            pl.BlockSpec((tm, LANES), const2),
            pl.BlockSpec((tm, LANES), const2),
            pl.BlockSpec((1, 8, LANES), lambda i: (i % tiles_per_seq, 0, 0)),
            pl.BlockSpec((1, 8, LANES), lambda i: (i % tiles_per_seq, 0, 0)),
        ],
        out_specs=([pl.BlockSpec((tm // radix, radix * F_WIDTH), row)]
                   + [pl.BlockSpec((tm, w), row) for w in widths]),
        out_shape=([jax.ShapeDtypeStruct((tokens // radix, radix * F_WIDTH), BF16)]
                   + [jax.ShapeDtypeStruct((tokens, w), BF16) for w in widths]),
        scratch_shapes=[pltpu.VMEM((N_FGROUPS, tm, FGROUP), F32)],
        compiler_params=pltpu.CompilerParams(
            dimension_semantics=("arbitrary",), vmem_limit_bytes=VMEM_LIMIT),
        name="in_projection",
    )(x2, gains, w_in, cr, sr, cb, sb)


def _dft_tables(seq, tr):
    n = _seq_dft(seq)
    radix = seq // n
    tiles = n // 2 // tr
    rows = tr + DFT_PAD
    k = np.arange(n // 2 + DFT_PAD, dtype=np.int64)
    m = np.arange(n, dtype=np.int64)
    ang = 2.0 * np.pi * ((k[:, None] * m[None, :]) % n).astype(np.float64) / n
    cs = np.stack([np.cos(ang), np.sin(ang)])
    cidx = np.arange(FGROUP, dtype=np.int64)
    cang = 2.0 * np.pi * ((cidx[:, None] * cidx[None, :]) % FGROUP).astype(np.float64) / FGROUP
    cc, sc = np.cos(cang), np.sin(cang)
    norm = 1.0 / math.sqrt(float(seq) * FGROUP)
    chan = np.block([[cc, cc], [sc, -sc]]) * norm
    rev_n = min(REV_BLOCK, tr)
    rev = np.zeros((rev_n, rev_n), np.float64)
    rev[np.arange(1, rev_n), rev_n - np.arange(1, rev_n)] = 1.0
    n1 = np.arange(radix, dtype=np.int64)
    i = np.arange(rows, dtype=np.int64)
    tang = 2.0 * np.pi * (n1[:, None] * i[None, :]).astype(np.float64) / seq
    tw = np.repeat(np.stack([np.cos(tang), np.sin(tang)])[..., None], LANES, axis=-1)
    t0 = np.arange(tiles, dtype=np.int64) * tr
    bang = 2.0 * np.pi * ((t0[:, None] * n1[None, :]) % seq).astype(np.float64) / seq
    base = np.stack([np.cos(bang), np.sin(bang)]).reshape(-1)
    return (jnp.asarray(cs, F32).astype(BF16), jnp.asarray(chan, F32).astype(BF16),
            jnp.asarray(rev, F32).astype(BF16), jnp.asarray(tw, F32), jnp.asarray(base, F32))


def _radix2_fft(re, im):
    n = len(re)
    if n == 1:
        return re, im
    er, ei = _radix2_fft(re[0::2], im[0::2])
    orr, oi = _radix2_fft(re[1::2], im[1::2])
    out_r = [None] * n
    out_i = [None] * n
    for k in range(n // 2):
        ang = -2.0 * math.pi * k / n
        c, s = math.cos(ang), math.sin(ang)
        if k == 0:
            tr, ti = orr[k], oi[k]
        elif 4 * k == n:
            tr, ti = oi[k], -orr[k]
        else:
            tr = orr[k] * c - oi[k] * s
            ti = orr[k] * s + oi[k] * c
        out_r[k] = er[k] + tr
        out_i[k] = ei[k] + ti
        out_r[k + n // 2] = er[k] - tr
        out_i[k + n // 2] = ei[k] - ti
    return out_r, out_i


def _fourier_kernel(radix, tr, tiles, nb, base_ref, cs_ref, chan_ref, rev_ref, tw_ref, u_ref,
                    zlo_ref, zhi_ref, lo_ref, hi_ref):
    t = pl.program_id(1)
    rows = tr + DFT_PAD
    r0 = pl.multiple_of(t * tr, tr)
    c = cs_ref[0, pl.ds(r0, rows), :]
    s = cs_ref[1, pl.ds(r0, rows), :]
    chan = chan_ref[...]
    rev = rev_ref[...]
    rev_n = rev.shape[0]
    pass_groups = N_FGROUPS if radix <= MAX_SHORT_RADIX else 2
    width = pass_groups * FGROUP
    first_row = lax.broadcasted_iota(jnp.int32, (8, width), 0) == 0
    for bb in range(nb):
        for g0 in range(0, N_FGROUPS, pass_groups):
            c0 = g0 * FGROUP
            re, im = [], []
            for n1 in range(radix):
                u = u_ref[bb, :, n1 * F_WIDTH + c0:n1 * F_WIDTH + c0 + width]
                p = jnp.dot(c, u, preferred_element_type=F32)
                q = jnp.dot(s, u, preferred_element_type=F32)
                if n1 == 0:
                    re.append(p)
                    im.append(-q)
                    continue
                ct, st = tw_ref[0, n1], tw_ref[1, n1]
                if tiles > 1:
                    cb = base_ref[t * radix + n1]
                    sb = base_ref[(tiles + t) * radix + n1]
                    ct, st = cb * ct - sb * st, sb * ct + cb * st
                ct = jnp.concatenate([ct] * pass_groups, axis=1)
                st = jnp.concatenate([st] * pass_groups, axis=1)
                re.append(p * ct - q * st)
                im.append(-(p * st + q * ct))
            xr, xi = _radix2_fft(re, im)
            for k1 in range(radix):
                xrb = xr[k1].astype(BF16)
                xib = xi[k1].astype(BF16)
                y_lo, y_hi = [], []
                for g in range(pass_groups):
                    lo = g * FGROUP
                    lhs = jnp.concatenate([xrb[:, lo:lo + FGROUP], xib[:, lo:lo + FGROUP]], axis=1)
                    z = jnp.dot(lhs, chan, preferred_element_type=F32)
                    y_lo.append(z[:tr, :FGROUP])
                    y_hi.append(z[:, FGROUP:])
                y_lo = jnp.concatenate(y_lo, axis=1)
                lo_ref[bb, k1, :, c0:c0 + width] = (
                    y_lo * zlo_ref[bb, k1, 0, :, c0:c0 + width].astype(F32)).astype(BF16)
                y_hi = jnp.concatenate(y_hi, axis=1).astype(BF16)
                km = radix - 1 - k1
                for b in range(tr // rev_n):
                    top = tr - rev_n * b
                    blk = jnp.dot(rev, y_hi[top - rev_n:top], preferred_element_type=F32)
                    head = jnp.where(first_row, y_hi[top:top + 16].astype(F32)[:8], blk[:8])
                    blk = jnp.concatenate([head, blk[8:]], axis=0)
                    z_hi = zhi_ref[bb, km, 0, rev_n * b:rev_n * (b + 1), c0:c0 + width].astype(F32)
                    hi_ref[bb, km, rev_n * b:rev_n * (b + 1), c0:c0 + width] = (blk * z_hi).astype(BF16)


def _fourier_mix(u_il, zf_silu, batch, seq):
    n = _seq_dft(seq)
    half = n // 2
    radix = seq // n
    tr = min(DFT_ROW_TILE, half) if radix <= MAX_SHORT_RADIX else RADIX_ROW_TILE
    tiles = half // tr
    cs, chan, rev, tw, base = _dft_tables(seq, tr)
    u3 = u_il.reshape(batch, n, radix * F_WIDTH)
    z5 = zf_silu.reshape(batch, radix, 2, half, F_WIDTH)
    u_mode = pl.Buffered(1) if batch == 1 else pl.Buffered(2)
    nb = FOURIER_BATCH if batch % FOURIER_BATCH == 0 else 1
    const = lambda nd: (lambda b, t: (0,) * nd)
    half_shape = jax.ShapeDtypeStruct((batch, radix, half, F_WIDTH), BF16)
    lo, hi = pl.pallas_call(
        functools.partial(_fourier_kernel, radix, tr, tiles, nb),
        grid=(batch // nb, tiles),
        in_specs=[
            pl.BlockSpec(memory_space=pltpu.SMEM),
            pl.BlockSpec(cs.shape, const(3), pipeline_mode=pl.Buffered(1)),
            pl.BlockSpec(chan.shape, const(2)),
            pl.BlockSpec(rev.shape, const(2)),
            pl.BlockSpec(tw.shape, const(4), pipeline_mode=pl.Buffered(1)),
            pl.BlockSpec((nb, n, radix * F_WIDTH), lambda b, t: (b, 0, 0), pipeline_mode=u_mode),
            pl.BlockSpec((nb, radix, 1, tr, F_WIDTH), lambda b, t: (b, 0, 0, t, 0)),
            pl.BlockSpec((nb, radix, 1, tr, F_WIDTH), lambda b, t: (b, 0, 1, tiles - 1 - t, 0)),
        ],
        out_specs=[
            pl.BlockSpec((nb, radix, tr, F_WIDTH), lambda b, t: (b, 0, t, 0)),
            pl.BlockSpec((nb, radix, tr, F_WIDTH), lambda b, t: (b, 0, tiles - 1 - t, 0)),
        ],
        out_shape=[half_shape, half_shape],
        compiler_params=pltpu.CompilerParams(
            dimension_semantics=("arbitrary", "arbitrary"), vmem_limit_bytes=VMEM_LIMIT),
        name="fourier_mix",
    )(base, cs, chan, rev, tw, u3, z5, z5)
    return lo.reshape(batch * radix, half, F_WIDTH), hi.reshape(batch * radix, half, F_WIDTH)


def _attn_out_kernel(final, layer, n_blocks, sink_ref, q_ref, kp_ref, kc_ref, kn_ref, vp_ref, vc_ref,
                     vn_ref, za_ref, ylo_ref, yhi_ref, x_ref, gain_ref, wg_ref, bias_ref, wpf_ref,
                     wpa_ref, wo_ref, fgain_ref, out_ref, ya_scratch, g_scratch):
    tm = TOKEN_TILE
    blocks = tm // BLOCK
    tiles_per_seq = n_blocks // blocks
    ti = pl.program_id(0) % tiles_per_seq
    x = x_ref[...]
    h = _rmsnorm_bf16(x, gain_ref[0])
    kcat = jnp.concatenate([kp_ref[...], kc_ref[...], kn_ref[...]], axis=0)
    vcat = jnp.concatenate([vp_ref[...], vc_ref[...], vn_ref[...]], axis=0)
    vcat_t = vcat.astype(F32).T.astype(BF16)
    ones_rows = jnp.ones((ONES_ROWS, 3 * BLOCK), BF16)
    d = (lax.broadcasted_iota(jnp.int32, (BLOCK, BLOCK), 0)
         - lax.broadcasted_iota(jnp.int32, (BLOCK, BLOCK), 1))
    d = jnp.concatenate([d] * N_HEADS, axis=1)
    gate_chunk = GATE_WIDTH // blocks

    def scores(j):
        cols = []
        for g in range(N_KV):
            kk = kcat[j * BLOCK:(j + 3) * BLOCK, g * HEAD_DIM:(g + 1) * HEAD_DIM]
            q4 = jnp.concatenate(
                [q_ref[j * BLOCK:(j + 1) * BLOCK, hd * HEAD_DIM:(hd + 1) * HEAD_DIM]
                 for hd in range(GROUP * g, GROUP * (g + 1))], axis=0)
            cols.append(lax.dot_general(kk, q4, (((1,), (1,)), ((), ())), preferred_element_type=F32))
        return jnp.concatenate(cols, axis=1)

    s_next = scores(0)
    for j in range(blocks):
        s = s_next
        if j + 1 < blocks:
            s_next = scores(j + 1)
        c0 = j * gate_chunk
        pre = jnp.dot(h, wg_ref[0, :, GATE_SKIP + c0:GATE_SKIP + c0 + gate_chunk],
                      preferred_element_type=F32)
        g_scratch[:, c0:c0 + gate_chunk] = jax.nn.sigmoid(
            pre + bias_ref[0, :, c0:c0 + gate_chunk]).astype(BF16)
        bj = ti * blocks + j
        lo = jnp.where(bj > 0, 0, BLOCK)
        hi = jnp.where(bj < n_blocks - 1, 0, -BLOCK)
        top = jnp.where(d >= lo, s[:BLOCK], NEG)
        mid = s[BLOCK:2 * BLOCK]
        bot = jnp.where(d <= hi, s[2 * BLOCK:], NEG)
        sk = jnp.concatenate(
            [jnp.full((1, BLOCK), sink_ref[layer * N_HEADS + hd] * LOG2E, F32) for hd in range(N_HEADS)],
            axis=1)
        m = jnp.maximum(jnp.maximum(jnp.max(top, axis=0, keepdims=True),
                                    jnp.max(mid, axis=0, keepdims=True)),
                        jnp.maximum(jnp.max(bot, axis=0, keepdims=True), sk))
        p = jnp.concatenate([jnp.exp2(top - m), jnp.exp2(mid - m), jnp.exp2(bot - m)],
                            axis=0).astype(BF16)
        p_sink = jnp.exp2(sk - m)
        for g in range(N_KV):
            v_t = vcat_t[g * HEAD_DIM:(g + 1) * HEAD_DIM, j * BLOCK:(j + 3) * BLOCK]
            v_aug = jnp.concatenate([v_t, ones_rows], axis=0)
            w = GROUP * BLOCK
            o_aug = jnp.dot(v_aug, p[:, g * w:(g + 1) * w], preferred_element_type=F32)
            denom = o_aug[HEAD_DIM:HEAD_DIM + 1] + p_sink[:, g * w:(g + 1) * w]
            o_t = o_aug[:HEAD_DIM] * (1.0 / denom)
            for hh in range(GROUP):
                hd = GROUP * g + hh
                o = o_t[:, hh * BLOCK:(hh + 1) * BLOCK].T
                za = za_ref[j * BLOCK:(j + 1) * BLOCK, hd * HEAD_DIM:(hd + 1) * HEAD_DIM]
                ya_scratch[j * BLOCK:(j + 1) * BLOCK, hd * HEAD_DIM:(hd + 1) * HEAD_DIM] = (
                    o * za.astype(F32)).astype(BF16)

    y_f = jnp.concatenate(
        [half_ref[b] for b in range(ylo_ref.shape[0]) for half_ref in (ylo_ref, yhi_ref)], axis=0)
    rows = tm // TAIL_CHUNKS
    for r0 in range(0, tm, rows):
        br_f = jnp.dot(y_f[r0:r0 + rows], wpf_ref[0], preferred_element_type=F32)
        br_a = jnp.dot(ya_scratch[r0:r0 + rows], wpa_ref[0], preferred_element_type=F32)
        merged = (g_scratch[r0:r0 + rows, :D_MODEL].astype(F32) * br_f
                  + g_scratch[r0:r0 + rows, D_MODEL:].astype(F32) * br_a).astype(BF16)
        y = x[r0:r0 + rows] + jnp.dot(merged, wo_ref[0], preferred_element_type=F32)
        if final:
            ms = jnp.mean(y * y, axis=-1, keepdims=True)
            y = (y * lax.rsqrt(ms + EPS)) * fgain_ref[...]
        out_ref[r0:r0 + rows] = y


def _attention_output(q, k, v, za_silu, yf_lo, yf_hi, x2, layer, sinks, gains, w_in, bias, w_pf, w_pa,
                      w_o, final_gain, final, seq):
    tokens = x2.shape[0]
    tm = TOKEN_TILE
    n = _seq_dft(seq)
    assert tm % n == 0 and yf_lo.shape == (tokens // n, n // 2, F_WIDTH)
    assert WINDOW == BLOCK
    per = tm // BLOCK
    n_blocks = seq // BLOCK
    tiles_per_seq = seq // tm
    row = lambda i: (i, 0)
    const2 = lambda i: (0, 0)
    of_layer = lambda i: (layer, 0, 0)
    half = pl.BlockSpec((tm // n, n // 2, F_WIDTH), lambda i: (i, 0, 0))

    def prev_block(i):
        return (jnp.maximum(i * per - 1, (i // tiles_per_seq) * n_blocks), 0)

    def next_block(i):
        return (jnp.minimum((i + 1) * per, (i // tiles_per_seq + 1) * n_blocks - 1), 0)

    prev = pl.BlockSpec((BLOCK, KV_WIDTH), prev_block)
    cur = pl.BlockSpec((tm, KV_WIDTH), row)
    nxt = pl.BlockSpec((BLOCK, KV_WIDTH), next_block)
    wide = pl.BlockSpec((tm, A_WIDTH), row)
    return pl.pallas_call(
        functools.partial(_attn_out_kernel, final, layer, n_blocks),
        grid=(tokens // tm,),
        in_specs=[
            pl.BlockSpec(memory_space=pltpu.SMEM),
            wide, prev, cur, nxt, prev, cur, nxt, wide,
            half, half,
            pl.BlockSpec((tm, D_MODEL), row),
            pl.BlockSpec((1, 1, D_MODEL), of_layer),
            pl.BlockSpec((1, D_MODEL, GATE_BLOCK), lambda i: (layer, 0, 1), pipeline_mode=pl.Buffered(1)),
            pl.BlockSpec((1, 1, GATE_WIDTH), of_layer),
            pl.BlockSpec((1, F_WIDTH, D_MODEL), of_layer, pipeline_mode=pl.Buffered(1)),
            pl.BlockSpec((1, A_WIDTH, D_MODEL), of_layer, pipeline_mode=pl.Buffered(1)),
            pl.BlockSpec((1, D_MODEL, D_MODEL), of_layer, pipeline_mode=pl.Buffered(1)),
            pl.BlockSpec((1, D_MODEL), const2),
        ],
        out_specs=pl.BlockSpec((tm, D_MODEL), row),
        out_shape=jax.ShapeDtypeStruct((tokens, D_MODEL), F32),
        scratch_shapes=[pltpu.VMEM((tm, A_WIDTH), BF16), pltpu.VMEM((tm, GATE_WIDTH), BF16)],
        compiler_params=pltpu.CompilerParams(
            dimension_semantics=("arbitrary",), vmem_limit_bytes=VMEM_LIMIT),
        name="attention_output",
    )(sinks, q, k, k, k, v, v, v, za_silu, yf_lo, yf_hi, x2, gains, w_in, bias, w_pf, w_pa, w_o,
      final_gain.reshape(1, D_MODEL))


def _trunk(x, gains, w_in, bias, sinks, w_pf, w_pa, w_o, final_gain):
    batch, seq, _ = x.shape
    depth = gains.shape[0]
    x2 = x.reshape(batch * seq, D_MODEL)
    for l in range(depth):
        uf, zf, q, k, v, za = _in_projection(x2, l, gains, w_in, seq)
        yf_lo, yf_hi = _fourier_mix(uf, zf, batch, seq)
        x2 = _attention_output(q, k, v, za, yf_lo, yf_hi, x2, l, sinks, gains, w_in, bias, w_pf, w_pa,
                               w_o, final_gain, l == depth - 1, seq)
    return x2.reshape(batch, seq, D_MODEL)


def _prepare(norm_gain, w_in, gate_bias, sink_logit, w_proj_fourier, w_proj_attn, w_out):
    depth = norm_gain.shape[0]
    return (norm_gain.reshape(depth, 1, D_MODEL),
            w_in.astype(BF16),
            gate_bias.reshape(depth, 1, GATE_WIDTH),
            sink_logit.reshape(depth * N_HEADS),
            w_proj_fourier.astype(BF16), w_proj_attn.astype(BF16), w_out.astype(BF16))


def kernel(x_prompt, x_sample, norm_gain, w_in, gate_bias, sink_logit, w_proj_fourier, w_proj_attn,
           w_out, final_norm_gain):
    params = _prepare(norm_gain, w_in, gate_bias, sink_logit, w_proj_fourier, w_proj_attn, w_out)
    return (_trunk(x_prompt, *params, final_norm_gain), _trunk(x_sample, *params, final_norm_gain))
```
